```python
import math
import jax, jax.numpy as jnp
from jax import lax
import numpy as np


D_MODEL = 2048
BATCH = 1
SEQ = 16384
DEPTH = 4

N_EVEN = (DEPTH + 1) // 2
N_ODD = DEPTH // 2
D_FF = 4 * D_MODEL
HEAD_DIM = 128
ROT_DIM = HEAD_DIM // 4
ROPE_THETA = 500000.0
D_RNN = D_MODEL // 2
RNN_BLOCKS = 16
RNN_BLOCK_W = D_RNN // RNN_BLOCKS
CONV_W = 4
RG_C = 8.0
H_DIL = (D_MODEL - D_RNN) // HEAD_DIM
D_DIL = H_DIL * HEAD_DIM
DIL_PATTERNS = ((128, 1), (512, 4), (2048, 16))
H_DIFF = D_MODEL // (2 * HEAD_DIM)
D_DIFF = H_DIFF * 2 * HEAD_DIM
EVEN_IN = 2 * D_RNN + 3 * D_DIL
ODD_IN = 3 * D_DIFF
BLK = 128
ALPHA = (2 * DEPTH) ** 0.25
BETA = (8 * DEPTH) ** -0.25
LN_EPS = 1e-5
RMS_EPS = 1e-5
NEG = -1e30

kernel_name = 'hybrid_rglru_dilated_diffattn_trunk'


def layer_norm(x, g, b):
    xf = x.astype(jnp.float32)
    mu = jnp.mean(xf, axis=-1, keepdims=True)
    var = jnp.mean(jnp.square(xf - mu), axis=-1, keepdims=True)
    y = (xf - mu) * lax.rsqrt(var + LN_EPS) * g.astype(jnp.float32) + b.astype(jnp.float32)
    return y.astype(x.dtype)


def rope_tables(positions):
    inv = ROPE_THETA ** (-jnp.arange(0, ROT_DIM, 2, dtype=jnp.float32) / ROT_DIM)
    ang = positions.astype(jnp.float32)[..., None] * inv
    return jnp.cos(ang), jnp.sin(ang)


def apply_rope(x, cos, sin):
    shape = cos.shape[:2] + (1,) * (x.ndim - 3) + (cos.shape[-1],)
    c, s = cos.reshape(shape), sin.reshape(shape)
    half = ROT_DIM // 2
    xr = x[..., :ROT_DIM].astype(jnp.float32)
    x1, x2 = xr[..., :half], xr[..., half:]
    rot = jnp.concatenate([x1 * c - x2 * s, x2 * c + x1 * s], axis=-1).astype(x.dtype)
    return jnp.concatenate([rot, x[..., ROT_DIM:]], axis=-1)


def causal_depthwise_conv(x, w, b):
    y = lax.conv_general_dilated(x, w[:, None, :], window_strides=(1,), padding=[(CONV_W - 1, 0)],
                                 dimension_numbers=('NWC', 'WIO', 'NWC'), feature_group_count=x.shape[-1])
    return y + b


def rg_lru(xc, ga_w, ga_b, gx_w, gx_b, lam):
    B, S, C = xc.shape
    xb = xc.reshape(B, S, RNN_BLOCKS, RNN_BLOCK_W)
    r = jax.nn.sigmoid(jnp.einsum('bsnc,ncd->bsnd', xb, ga_w).reshape(B, S, C) + ga_b)
    i = jax.nn.sigmoid(jnp.einsum('bsnc,ncd->bsnd', xb, gx_w).reshape(B, S, C) + gx_b)
    log_a = (-RG_C * r.astype(jnp.float32)) * jax.nn.softplus(-lam.astype(jnp.float32))
    a = jnp.exp(log_a)
    u = jnp.sqrt(-jnp.expm1(2.0 * log_a)) * (i * xc).astype(jnp.float32)

    def combine(c1, c2):
        a1, b1 = c1
        a2, b2 = c2
        return a1 * a2, a2 * b1 + b2

    _, h = lax.associative_scan(combine, (a, u), axis=1)
    return h.astype(xc.dtype)


def to_strided(t, d):
    B, S = t.shape[:2]
    rest = t.shape[2:]
    return jnp.swapaxes(t.reshape((B, S // d, d) + rest), 1, 2).reshape((B * d, S // d) + rest)


def from_strided(t, d, B):
    L = t.shape[1]
    rest = t.shape[2:]
    return jnp.swapaxes(t.reshape((B, d, L) + rest), 1, 2).reshape((B, L * d) + rest)


def banded_window_attention(q, k, v, win_keys):
    N, L, H, D = q.shape
    nb = -(-L // BLK)
    Lp = nb * BLK
    qb = jnp.pad(q, ((0, 0), (0, Lp - L), (0, 0), (0, 0))).reshape(N, nb, BLK, H, D)

    def kv_blocks(t):
        tp = jnp.pad(t, ((0, 0), (BLK, Lp - L), (0, 0), (0, 0))).reshape(N, nb + 1, BLK, H, D)
        return jnp.concatenate([tp[:, :-1], tp[:, 1:]], axis=2)

    kb, vb = kv_blocks(k), kv_blocks(v)
    s = jnp.einsum('nbqhd,nbkhd->nbhqk', qb, kb).astype(jnp.float32) * (D ** -0.5)
    qi = jnp.arange(BLK)[:, None]
    kj = jnp.arange(2 * BLK)[None, :]
    dist = qi + BLK - kj
    kpos = jnp.arange(nb)[:, None, None] * BLK - BLK + kj[None]
    valid = (dist >= 0) & (dist <= win_keys) & (kpos >= 0)
    s = jnp.where(valid[None, :, None], s, NEG)
    lse = jax.nn.logsumexp(s, axis=-1)
    p = jnp.exp(s - lse[..., None])
    o = jnp.einsum('nbhqk,nbkhd->nbqhd', p.astype(v.dtype), vb).reshape(N, Lp, H, D)[:, :L]
    lse = jnp.transpose(lse, (0, 1, 3, 2)).reshape(N, Lp, H)[:, :L]
    return o, lse


def dilated_window_attention(q, k, v):
    B = q.shape[0]
    outs, lses = [], []
    for window, dil in DIL_PATTERNS:
        o, lse = banded_window_attention(to_strided(q, dil), to_strided(k, dil), to_strided(v, dil), window // dil)
        outs.append(from_strided(o, dil, B))
        lses.append(from_strided(lse, dil, B))
    w = jax.nn.softmax(jnp.stack(lses, axis=0), axis=0)
    o = jnp.einsum('pbsh,pbshd->bshd', w, jnp.stack(outs, axis=0).astype(jnp.float32))
    return o.astype(q.dtype)


def even_mixer(x, cos, sin, w_in, conv_w, conv_b, ga_w, ga_b, gx_w, gx_b, rg_lam, w_out):
    B, S, _ = x.shape
    proj = x @ w_in
    a_gate, a_x, q, k, v = jnp.split(proj, [D_RNN, 2 * D_RNN, 2 * D_RNN + D_DIL, 2 * D_RNN + 2 * D_DIL], axis=-1)
    h = rg_lru(causal_depthwise_conv(a_x, conv_w, conv_b), ga_w, ga_b, gx_w, gx_b, rg_lam)
    ya = jax.nn.gelu(a_gate) * h
    q = apply_rope(q.reshape(B, S, H_DIL, HEAD_DIM), cos, sin)
    k = apply_rope(k.reshape(B, S, H_DIL, HEAD_DIM), cos, sin)
    v = v.reshape(B, S, H_DIL, HEAD_DIM)
    yb = dilated_window_attention(q, k, v).reshape(B, S, D_DIL)
    return jnp.concatenate([ya, yb], axis=-1) @ w_out


def causal_diff_attention(q, k, v, lam):
    B, S, H = q.shape[:3]
    nb = S // BLK
    qb = jnp.swapaxes(q.reshape(B, nb, BLK, H, 2, HEAD_DIM), 0, 1)
    kpos = jnp.arange(S)

    def one_block(args):
        qblk, start = args
        s = jnp.einsum('bqhmd,bkhmd->bmhqk', qblk, k).astype(jnp.float32) * (HEAD_DIM ** -0.5)
        qpos = start + jnp.arange(BLK)
        mask = kpos[None, :] <= qpos[:, None]
        p = jax.nn.softmax(jnp.where(mask, s, NEG), axis=-1)
        pd = p[:, 0] - lam * p[:, 1]
        return jnp.einsum('bhqk,bkhe->bqhe', pd.astype(v.dtype), v)

    o = lax.map(one_block, (qb, jnp.arange(nb) * BLK))
    return jnp.swapaxes(o, 0, 1).reshape(B, S, H, 2 * HEAD_DIM)


def odd_mixer(x, cos, sin, w_in, lq1, lk1, lq2, lk2, subln_g, w_out, lambda_init):
    B, S, _ = x.shape
    q, k, v = jnp.split(x @ w_in, 3, axis=-1)
    q = apply_rope(q.reshape(B, S, H_DIFF, 2, HEAD_DIM), cos, sin)
    k = apply_rope(k.reshape(B, S, H_DIFF, 2, HEAD_DIM), cos, sin)
    v = v.reshape(B, S, H_DIFF, 2 * HEAD_DIM)
    lam = (jnp.exp(jnp.sum(lq1.astype(jnp.float32) * lk1.astype(jnp.float32)))
           - jnp.exp(jnp.sum(lq2.astype(jnp.float32) * lk2.astype(jnp.float32))) + lambda_init)
    o = causal_diff_attention(q, k, v, lam).astype(jnp.float32)
    o = o * lax.rsqrt(jnp.mean(jnp.square(o), axis=-1, keepdims=True) + RMS_EPS) * subln_g.astype(jnp.float32)
    o = (o * (1.0 - lambda_init)).astype(x.dtype)
    return o.reshape(B, S, D_DIFF) @ w_out


def squared_relu_mlp(x, w1, w2):
    h = jax.nn.relu(x @ w1)
    return (h * h) @ w2


def setup_inputs(seed: int = 0) -> dict:
    key = jax.random.key(seed)
    ks = jax.random.split(key, 24)
    f32 = jnp.float32

    def nrm(k, shape, scale):
        return jax.random.normal(k, shape, f32) * scale

    u = jax.random.uniform(ks[9], (N_EVEN, D_RNN), f32, 0.9, 0.999)
    sg = u ** (1.0 / RG_C)
    rg_lambda = jnp.log(sg) - jnp.log1p(-sg)
    return {
        'x': nrm(ks[0], (BATCH, SEQ, D_MODEL), 1.0),
        'positions': jnp.broadcast_to(jnp.arange(SEQ, dtype=jnp.int32), (BATCH, SEQ)),
        'ev_w_in': nrm(ks[1], (N_EVEN, D_MODEL, EVEN_IN), D_MODEL ** -0.5),
        'ev_conv_w': nrm(ks[2], (N_EVEN, CONV_W, D_RNN), CONV_W ** -0.5),
        'ev_conv_b': nrm(ks[3], (N_EVEN, D_RNN), 0.01),
        'ev_gate_a_w': nrm(ks[4], (N_EVEN, RNN_BLOCKS, RNN_BLOCK_W, RNN_BLOCK_W), RNN_BLOCK_W ** -0.5),
        'ev_gate_a_b': nrm(ks[5], (N_EVEN, D_RNN), 0.01),
        'ev_gate_x_w': nrm(ks[6], (N_EVEN, RNN_BLOCKS, RNN_BLOCK_W, RNN_BLOCK_W), RNN_BLOCK_W ** -0.5),
        'ev_gate_x_b': nrm(ks[7], (N_EVEN, D_RNN), 0.01),
        'ev_rg_lambda': rg_lambda,
        'ev_w_out': nrm(ks[8], (N_EVEN, D_RNN + D_DIL, D_MODEL), (D_RNN + D_DIL) ** -0.5 * BETA),
        'od_w_in': nrm(ks[10], (N_ODD, D_MODEL, ODD_IN), D_MODEL ** -0.5),
        'od_lambda_q1': nrm(ks[11], (N_ODD, HEAD_DIM), 0.1),
        'od_lambda_k1': nrm(ks[12], (N_ODD, HEAD_DIM), 0.1),
        'od_lambda_q2': nrm(ks[13], (N_ODD, HEAD_DIM), 0.1),
        'od_lambda_k2': nrm(ks[14], (N_ODD, HEAD_DIM), 0.1),
        'od_subln_g': 1.0 + nrm(ks[15], (N_ODD, 2 * HEAD_DIM), 0.02),
        'od_w_out': nrm(ks[16], (N_ODD, D_DIFF, D_MODEL), D_DIFF ** -0.5 * BETA),
        'ln_mix_g': 1.0 + nrm(ks[17], (DEPTH, D_MODEL), 0.02),
        'ln_mix_b': nrm(ks[18], (DEPTH, D_MODEL), 0.01),
        'ln_mlp_g': 1.0 + nrm(ks[19], (DEPTH, D_MODEL), 0.02),
        'ln_mlp_b': nrm(ks[20], (DEPTH, D_MODEL), 0.01),
        'mlp_w1': nrm(ks[21], (DEPTH, D_MODEL, D_FF), D_MODEL ** -0.5),
        'mlp_w2': nrm(ks[22], (DEPTH, D_FF, D_MODEL), D_FF ** -0.5 * BETA),
    }


def reference(x, positions, ev_w_in, ev_conv_w, ev_conv_b, ev_gate_a_w, ev_gate_a_b, ev_gate_x_w,
              ev_gate_x_b, ev_rg_lambda, ev_w_out, od_w_in, od_lambda_q1, od_lambda_k1, od_lambda_q2,
              od_lambda_k2, od_subln_g, od_w_out, ln_mix_g, ln_mix_b, ln_mlp_g, ln_mlp_b, mlp_w1, mlp_w2):
    cos, sin = rope_tables(positions)
    for layer in range(DEPTH):
        p = layer // 2
        if layer % 2 == 0:
            mix = even_mixer(x, cos, sin, ev_w_in[p], ev_conv_w[p], ev_conv_b[p], ev_gate_a_w[p], ev_gate_a_b[p],
                             ev_gate_x_w[p], ev_gate_x_b[p], ev_rg_lambda[p], ev_w_out[p])
        else:
            lambda_init = 0.8 - 0.6 * math.exp(-0.3 * layer)
            mix = odd_mixer(x, cos, sin, od_w_in[p], od_lambda_q1[p], od_lambda_k1[p], od_lambda_q2[p],
                            od_lambda_k2[p], od_subln_g[p], od_w_out[p], lambda_init)
        x = layer_norm(ALPHA * x + mix, ln_mix_g[layer], ln_mix_b[layer])
        x = layer_norm(ALPHA * x + squared_relu_mlp(x, mlp_w1[layer], mlp_w2[layer]), ln_mlp_g[layer], ln_mlp_b[layer])
    return x
```

```python
import functools
import math

import jax
import jax.numpy as jnp
from jax import lax
from jax.experimental import pallas as pl
from jax.experimental.pallas import tpu as pltpu

HEAD_DIM = 128
ROT_DIM = HEAD_DIM // 4
ROT_HALF = ROT_DIM // 2
ROPE_THETA = 500000.0
CONV_W = 4
RG_C = 8.0
RNN_BLOCK_W = 64
DIL_PATTERNS = ((128, 1), (512, 4), (2048, 16))
BLK = 128
LN_EPS = 1e-5
RMS_EPS = 1e-5
NEG = -1e30
LOG2E = 1.4426950408889634

V7X_VMEM_BYTES = 64 * 2**20
V7X_VMEM_RESERVED_BYTES = 8 * 2**20
V7X_SUBLANES = 8
V7X_MXU_DIM = 256

F32 = jnp.float32
BF16 = jnp.bfloat16


def _params(semantics, vmem_bytes):
    limit = min(int(vmem_bytes), V7X_VMEM_BYTES - V7X_VMEM_RESERVED_BYTES)
    return pltpu.CompilerParams(dimension_semantics=semantics, vmem_limit_bytes=limit)


def _nbytes(shape, dtype):
    return math.prod(shape) * jnp.dtype(dtype).itemsize


def _ln_rows(z, g, b):
    mu = jnp.mean(z, axis=-1, keepdims=True)
    zc = z - mu
    var = jnp.mean(zc * zc, axis=-1, keepdims=True)
    return zc * lax.rsqrt(var + LN_EPS) * g + b


def _proj_kernel(x_ref, w_ref, o_ref):
    o_ref[...] = jnp.dot(x_ref[...], w_ref[...], preferred_element_type=F32).astype(o_ref.dtype)


def _proj_rope_kernel(x_ref, w_ref, c_ref, s1_ref, s2_ref, o_ref, *, n_q, n_rope, q_scale):
    j = pl.program_id(1)
    acc = jnp.dot(x_ref[...], w_ref[...], preferred_element_type=F32)

    @pl.when(j < n_rope)
    def _():
        scale = jnp.where(j < n_q, jnp.float32(q_scale), jnp.float32(1.0))
        c, s1, s2 = c_ref[...], s1_ref[...], s2_ref[...]
        for h in range(acc.shape[1] // HEAD_DIM):
            cols = slice(h * HEAD_DIM, (h + 1) * HEAD_DIM)
            a = acc[:, cols]
            r = a * c + pltpu.roll(a, ROT_HALF, 1) * s1 + pltpu.roll(a, HEAD_DIM - ROT_HALF, 1) * s2
            o_ref[:, cols] = (r * scale).astype(o_ref.dtype)

    @pl.when(j >= n_rope)
    def _():
        o_ref[...] = acc.astype(o_ref.dtype)


def _proj_tiles(m, n):
    tm = min(1024, m)
    tn = min(1024, n)
    assert m % tm == 0 and n % tn == 0
    return tm, tn


def _proj(xb, w, out_dtype):
    m, k = xb.shape
    n = w.shape[1]
    tm, tn = _proj_tiles(m, n)
    vmem = 2 * (_nbytes((tm, k), BF16) + _nbytes((k, tn), BF16) + _nbytes((tm, tn), out_dtype)) \
        + 2 * _nbytes((tm, tn), F32)
    return pl.pallas_call(
        _proj_kernel,
        grid=(m // tm, n // tn),
        in_specs=[pl.BlockSpec((tm, k), lambda i, j: (i, 0)),
                  pl.BlockSpec((k, tn), lambda i, j: (0, j))],
        out_specs=pl.BlockSpec((tm, tn), lambda i, j: (i, j)),
        out_shape=jax.ShapeDtypeStruct((m, n), out_dtype),
        compiler_params=_params(("parallel", "arbitrary"), vmem),
        name="proj",
    )(xb, w)


def _proj_rope(xb, w, rope_tabs, n_q_cols, n_rope_cols, q_scale):
    m, k = xb.shape
    n = w.shape[1]
    tm, tn = _proj_tiles(m, n)
    assert n_q_cols % tn == 0 and n_rope_cols % tn == 0
    tab_spec = pl.BlockSpec((tm, HEAD_DIM), lambda i, j: (i, 0))
    vmem = 2 * (_nbytes((tm, k), BF16) + _nbytes((k, tn), BF16) + _nbytes((tm, tn), BF16)
                + 3 * _nbytes((tm, HEAD_DIM), F32)) + 3 * _nbytes((tm, tn), F32)
    kern = functools.partial(_proj_rope_kernel, n_q=n_q_cols // tn, n_rope=n_rope_cols // tn, q_scale=q_scale)
    return pl.pallas_call(
        kern,
        grid=(m // tm, n // tn),
        in_specs=[pl.BlockSpec((tm, k), lambda i, j: (i, 0)),
                  pl.BlockSpec((k, tn), lambda i, j: (0, j)),
                  tab_spec, tab_spec, tab_spec],
        out_specs=pl.BlockSpec((tm, tn), lambda i, j: (i, j)),
        out_shape=jax.ShapeDtypeStruct((m, n), BF16),
        compiler_params=_params(("parallel", "arbitrary"), vmem),
        name="proj_rope",
    )(xb, w, *rope_tabs)


def _outproj_ln_kernel(*refs, n_in, alpha):
    ys, ws = refs[:n_in], refs[n_in:2 * n_in]
    x_ref, g_ref, b_ref, of_ref, ob_ref = refs[2 * n_in:]
    acc = jnp.dot(ys[0][...], ws[0][...], preferred_element_type=F32)
    for y_ref, w_ref in zip(ys[1:], ws[1:]):
        acc = acc + jnp.dot(y_ref[...], w_ref[...], preferred_element_type=F32)
    y = _ln_rows(alpha * x_ref[...] + acc, g_ref[...], b_ref[...])
    of_ref[...] = y
    ob_ref[...] = y.astype(BF16)


def _outproj_ln(ys, ws, xf, g, b, alpha):
    m, d = xf.shape
    tm = min(512, m)
    assert m % tm == 0
    row = lambda i: (i, 0)
    const = lambda i: (0, 0)
    in_specs = [pl.BlockSpec((tm, y.shape[1]), row) for y in ys]
    in_specs += [pl.BlockSpec(w.shape, const) for w in ws]
    in_specs += [pl.BlockSpec((tm, d), row), pl.BlockSpec((1, d), const), pl.BlockSpec((1, d), const)]
    vmem = 2 * (sum(_nbytes((tm, y.shape[1]), BF16) for y in ys) + sum(_nbytes(w.shape, BF16) for w in ws)
                + 2 * _nbytes((tm, d), F32) + _nbytes((tm, d), BF16)) + 3 * _nbytes((tm, d), F32)
    return pl.pallas_call(
        functools.partial(_outproj_ln_kernel, n_in=len(ys), alpha=alpha),
        grid=(m // tm,),
        in_specs=in_specs,
        out_specs=[pl.BlockSpec((tm, d), row), pl.BlockSpec((tm, d), row)],
        out_shape=[jax.ShapeDtypeStruct((m, d), F32), jax.ShapeDtypeStruct((m, d), BF16)],
        compiler_params=_params(("parallel",), vmem),
        name="outproj_ln",
    )(*ys, *ws, xf, g, b)


def _mlp_ln_kernel(xb_ref, w1_ref, w2_ref, x_ref, g_ref, b_ref, of_ref, ob_ref, acc_ref, *, alpha):
    j = pl.program_id(1)

    @pl.when(j == 0)
    def _():
        acc_ref[...] = jnp.zeros_like(acc_ref)

    h = jnp.maximum(jnp.dot(xb_ref[...], w1_ref[...], preferred_element_type=F32), 0.0)
    acc_ref[...] += jnp.dot((h * h).astype(BF16), w2_ref[...], preferred_element_type=F32)

    @pl.when(j == pl.num_programs(1) - 1)
    def _():
        y = _ln_rows(alpha * x_ref[...] + acc_ref[...], g_ref[...], b_ref[...])
        of_ref[...] = y
        ob_ref[...] = y.astype(BF16)


def _mlp_ln(xb, w1, w2, xf, g, b, alpha):
    m, d = xf.shape
    f = w1.shape[1]
    tm = min(512, m)
    tf = min(1024, f)
    assert m % tm == 0 and f % tf == 0
    row = lambda i, j: (i, 0)
    const = lambda i, j: (0, 0)
    vmem = 2 * (_nbytes((tm, d), BF16) + 2 * _nbytes((d, tf), BF16) + 2 * _nbytes((tm, d), F32)
                + _nbytes((tm, d), BF16)) + 2 * _nbytes((tm, d), F32) + 2 * _nbytes((tm, tf), F32)
    return pl.pallas_call(
        functools.partial(_mlp_ln_kernel, alpha=alpha),
        grid=(m // tm, f // tf),
        in_specs=[pl.BlockSpec((tm, d), row),
                  pl.BlockSpec((d, tf), lambda i, j: (0, j)),
                  pl.BlockSpec((tf, d), lambda i, j: (j, 0)),
                  pl.BlockSpec((tm, d), row),
                  pl.BlockSpec((1, d), const), pl.BlockSpec((1, d), const)],
        out_specs=[pl.BlockSpec((tm, d), row), pl.BlockSpec((tm, d), row)],
        out_shape=[jax.ShapeDtypeStruct((m, d), F32), jax.ShapeDtypeStruct((m, d), BF16)],
        scratch_shapes=[pltpu.VMEM((tm, d), F32)],
        compiler_params=_params(("parallel", "arbitrary"), vmem),
        name="mlp_ln",
    )(xb, w1, w2, xf, g, b)


def _rglru_kernel(ag_ref, ax_ref, cw_ref, cb_ref, wa_ref, wx_ref, ba_ref, bx_ref, lam_ref,
                  o_ref, xs_ref, a_ref, b_ref, h_ref):
    i = pl.program_id(0)
    tm, c = ax_ref.shape
    sub = V7X_SUBLANES
    gw = wa_ref.shape[1]

    @pl.when(i == 0)
    def _():
        xs_ref[0:sub, :] = jnp.zeros((sub, c), F32)
        h_ref[...] = jnp.zeros_like(h_ref)

    x = ax_ref[...]
    xs_ref[sub:sub + tm, :] = x
    xc = cb_ref[...] + cw_ref[CONV_W - 1:CONV_W, :] * x
    for k in range(1, CONV_W):
        xc = xc + cw_ref[CONV_W - 1 - k:CONV_W - k, :] * xs_ref[pl.ds(sub - k, tm), :]
    xs_ref[0:sub, :] = x[tm - sub:tm, :]

    xcb = xc.astype(BF16)
    row_in_tile = lax.broadcasted_iota(jnp.int32, (tm, gw), 0) % sub
    for g in range(c // gw):
        cols = slice(g * gw, (g + 1) * gw)
        r = jax.nn.sigmoid(jnp.dot(xcb[:, cols], wa_ref[g], preferred_element_type=F32) + ba_ref[:, cols])
        ig = jax.nn.sigmoid(jnp.dot(xcb[:, cols], wx_ref[g], preferred_element_type=F32) + bx_ref[:, cols])
        log_a = (-RG_C * r) * jax.nn.softplus(-lam_ref[:, cols])
        a = jnp.exp(log_a)
        u = jnp.sqrt(-jnp.tanh(log_a) * (a * a + 1.0)) * (ig * xc[:, cols])
        for s in (1, 2, 4):
            keep = row_in_tile >= s
            u = jnp.where(keep, a * pltpu.roll(u, s, 0) + u, u)
            a = jnp.where(keep, a * pltpu.roll(a, s, 0), a)
        a_ref[:, cols] = a
        b_ref[:, cols] = u

    def tile_body(t, h):
        rows = pl.ds(pl.multiple_of(t * sub, sub), sub)
        hr = a_ref[rows, :] * h + b_ref[rows, :]
        b_ref[rows, :] = hr
        return jnp.broadcast_to(hr[sub - 1:sub, :], hr.shape)

    h_ref[...] = lax.fori_loop(0, tm // sub, tile_body, h_ref[...], unroll=4)

    gate = ag_ref[...]
    cdf = 0.5 * (1.0 + jnp.tanh(math.sqrt(2.0 / math.pi) * (gate + 0.044715 * (gate * gate * gate))))
    o_ref[...] = (gate * cdf * b_ref[...]).astype(o_ref.dtype)


def _rglru(ga, conv_w, conv_b, wa, wx, ba, bx, lam):
    m = ga.shape[0]
    c = ga.shape[1] // 2
    tm = min(256, m)
    assert m % tm == 0
    gw = wa.shape[1]
    const2 = lambda i: (0, 0)
    vmem = 2 * (2 * _nbytes((tm, c), F32) + _nbytes((tm, c), BF16) + 2 * _nbytes(wa.shape, BF16)) \
        + 3 * _nbytes((tm + 8, c), F32) + 12 * _nbytes((tm, c), F32)
    return pl.pallas_call(
        _rglru_kernel,
        grid=(m // tm,),
        in_specs=[pl.BlockSpec((tm, c), lambda i: (i, 0)),
                  pl.BlockSpec((tm, c), lambda i: (i, 1)),
                  pl.BlockSpec((CONV_W, c), const2), pl.BlockSpec((1, c), const2),
                  pl.BlockSpec(wa.shape, lambda i: (0, 0, 0)), pl.BlockSpec(wx.shape, lambda i: (0, 0, 0)),
                  pl.BlockSpec((1, c), const2), pl.BlockSpec((1, c), const2), pl.BlockSpec((1, c), const2)],
        out_specs=pl.BlockSpec((tm, c), lambda i: (i, 0)),
        out_shape=jax.ShapeDtypeStruct((m, c), BF16),
        scratch_shapes=[pltpu.VMEM((tm + V7X_SUBLANES, c), F32), pltpu.VMEM((tm, c), F32),
                        pltpu.VMEM((tm, c), F32), pltpu.VMEM((V7X_SUBLANES, c), F32)],
        compiler_params=_params(("arbitrary",), vmem),
        name="rglru",
    )(ga, ga, conv_w, conv_b, wa, wx, ba, bx, lam)


def _block_diag_groups(w, group):
    n, bw, _ = w.shape
    per = group // bw
    eye = jnp.eye(per, dtype=w.dtype)
    wg = w.reshape(n // per, per, bw, bw)
    return jnp.einsum('gpab,pq->gpaqb', wg, eye).reshape(n // per, group, group)


def _dil_kernel(*refs, has_prev, is_last):
    q_ref, kc_ref, kp_ref, vc_ref, vp_ref = refs[:5]
    refs = refs[5:]
    if has_prev:
        op_ref, lp_ref = refs[:2]
        refs = refs[2:]
    o_ref = refs[0]
    l_ref = None if is_last else refs[1]

    i = pl.program_id(1)
    tq = q_ref.shape[0]
    qi = lax.broadcasted_iota(jnp.int32, (BLK, 2 * BLK), 0)
    kj = lax.broadcasted_iota(jnp.int32, (BLK, 2 * BLK), 1)
    band = (kj >= qi) & (kj <= qi + BLK)
    contract_last = (((1,), (1,)), ((), ()))
    for b in range(tq // BLK):
        rows = slice(b * BLK, (b + 1) * BLK)
        valid = band & (kj >= jnp.where(i > 0, 0, BLK)) if b == 0 else band
        bias = jnp.where(valid, 0.0, NEG)
        for h in range(q_ref.shape[1] // HEAD_DIM):
            cols = slice(h * HEAD_DIM, (h + 1) * HEAD_DIM)
            if b == 0:
                kk = jnp.concatenate([kp_ref[:, cols], kc_ref[0:BLK, cols]], axis=0)
                vv = jnp.concatenate([vp_ref[:, cols], vc_ref[0:BLK, cols]], axis=0)
            else:
                kk = kc_ref[(b - 1) * BLK:(b + 1) * BLK, cols]
                vv = vc_ref[(b - 1) * BLK:(b + 1) * BLK, cols]
            s = lax.dot_general(q_ref[rows, cols], kk, contract_last, preferred_element_type=F32) + bias
            m = jnp.max(s, axis=-1, keepdims=True)
            p = jnp.exp2(s - m)
            l = jnp.sum(p, axis=-1, keepdims=True)
            o = jnp.dot(p.astype(BF16), vv, preferred_element_type=F32) / l
            lse = jnp.broadcast_to(m + jnp.log2(l), (BLK, HEAD_DIM))
            if has_prev:
                lp = lp_ref[rows, cols]
                mx = jnp.maximum(lp, lse)
                wa = jnp.exp2(lp - mx)
                wb = jnp.exp2(lse - mx)
                tot = wa + wb
                o = (op_ref[rows, cols] * wa + o * wb) / tot
                lse = mx + jnp.log2(tot)
            o_ref[rows, cols] = o.astype(o_ref.dtype)
            if not is_last:
                l_ref[rows, cols] = lse


def _dilated_attention(qkv, col_blocks):
    s_len = qkv.shape[0]
    c = qkv.shape[1] // 3
    assert col_blocks == 3
    o = lse = None
    for idx, (window, dil) in enumerate(DIL_PATTERNS):
        assert window // dil == BLK
        has_prev, is_last = idx > 0, idx == len(DIL_PATTERNS) - 1
        length = s_len // dil
        tq = min(512, length)
        assert length % tq == 0 and tq % BLK == 0
        per = tq // BLK
        qkv_v = qkv.reshape(length, dil * 3 * c)
        cur = lambda off: pl.BlockSpec((tq, c), lambda r, i, off=off: (i, 3 * r + off))
        prev = lambda off: pl.BlockSpec((BLK, c), lambda r, i, off=off: (jnp.maximum(i * per - 1, 0), 3 * r + off))
        nat = pl.BlockSpec((tq, c), lambda r, i: (i, r))
        in_specs = [cur(0), cur(1), prev(1), cur(2), prev(2)]
        args = [qkv_v] * 5
        if has_prev:
            in_specs += [nat, nat]
            args += [o.reshape(length, dil * c), lse.reshape(length, dil * c)]
        out_dtype = BF16 if is_last else F32
        out_specs = [nat] if is_last else [nat, nat]
        out_shape = [jax.ShapeDtypeStruct((length, dil * c), out_dtype)]
        if not is_last:
            out_shape.append(jax.ShapeDtypeStruct((length, dil * c), F32))
        vmem = 2 * (3 * _nbytes((tq, c), BF16) + 2 * _nbytes((BLK, c), BF16) + 4 * _nbytes((tq, c), F32)) \
            + 16 * _nbytes((BLK, 2 * BLK), F32) * (c // HEAD_DIM)
        res = pl.pallas_call(
            functools.partial(_dil_kernel, has_prev=has_prev, is_last=is_last),
            grid=(dil, length // tq),
            in_specs=in_specs,
            out_specs=out_specs,
            out_shape=out_shape,
            compiler_params=_params(("parallel", "parallel"), vmem),
            name=f"dilated_attn_d{dil}",
        )(*args)
        o = res[0].reshape(s_len, c)
        lse = None if is_last else res[1].reshape(s_len, c)
    return o


def _diff_attn_kernel(q_ref, k_ref, v_ref, lq1_ref, lk1_ref, lq2_ref, lk2_ref, g_ref, o_ref,
                      m_ref, l_ref, acc_ref, *, lambda_init, tk):
    i = pl.program_id(1)
    tq = q_ref.shape[0]
    assert tq == tk
    reps = tk // HEAD_DIM
    m_ref[...] = jnp.full_like(m_ref, NEG)
    l_ref[...] = jnp.zeros_like(l_ref)
    acc_ref[...] = jnp.zeros_like(acc_ref)
    contract_last = (((1,), (1,)), ((), ()))

    def block(j, masked):
        rows = pl.ds(pl.multiple_of(j * tk, tk), tk)
        kb = k_ref[rows, :]
        vb = v_ref[rows, :]
        for c in range(2):
            cols = slice(c * HEAD_DIM, (c + 1) * HEAD_DIM)
            s = lax.dot_general(q_ref[:, cols], kb[:, cols], contract_last, preferred_element_type=F32)
            if masked:
                qi = lax.broadcasted_iota(jnp.int32, (tq, tk), 0)
                kj = lax.broadcasted_iota(jnp.int32, (tq, tk), 1)
                s = jnp.where(kj <= qi, s, NEG)
            m_prev = m_ref[c]
            m_new = jnp.maximum(m_prev, jnp.max(s, axis=-1, keepdims=True))
            alpha = jnp.exp2(m_prev - m_new)
            p = jnp.exp2(s - jnp.concatenate([m_new] * reps, axis=1))
            l_ref[c] = alpha * l_ref[c] + jnp.sum(p, axis=-1, keepdims=True)
            m_ref[c] = m_new
            pv = jnp.dot(p.astype(BF16), vb, preferred_element_type=F32)
            acc_ref[c] = acc_ref[c] * jnp.concatenate([alpha, alpha], axis=1) + pv

    def body(j, carry):
        block(j, False)
        return carry

    lax.fori_loop(0, i, body, 0)
    block(i, True)

    lam = (jnp.exp(jnp.sum(lq1_ref[...] * lk1_ref[...], axis=-1, keepdims=True))
           - jnp.exp(jnp.sum(lq2_ref[...] * lk2_ref[...], axis=-1, keepdims=True)) + lambda_init)
    inv1 = 1.0 / l_ref[0]
    inv2 = 1.0 / l_ref[1]
    o = acc_ref[0] * jnp.concatenate([inv1, inv1], axis=1) \
        - lam * (acc_ref[1] * jnp.concatenate([inv2, inv2], axis=1))
    o = o * lax.rsqrt(jnp.mean(o * o, axis=-1, keepdims=True) + RMS_EPS) * g_ref[...]
    o_ref[...] = (o * (1.0 - lambda_init)).astype(o_ref.dtype)


def _diff_attention(qkv, lq1, lk1, lq2, lk2, g, lambda_init):
    s_len = qkv.shape[0]
    d = qkv.shape[1] // 3
    hw = 2 * HEAD_DIM
    n_heads = d // hw
    tq = tk = min(512, s_len)
    assert s_len % tq == 0
    vec = pl.BlockSpec((1, HEAD_DIM), lambda h, i: (0, 0))
    vmem = 2 * (2 * _nbytes((s_len, hw), BF16) + 2 * _nbytes((tq, hw), BF16)) \
        + 2 * (2 * _nbytes((tq, HEAD_DIM), F32) + _nbytes((tq, hw), F32)) + 8 * _nbytes((tq, tk), F32)
    return pl.pallas_call(
        functools.partial(_diff_attn_kernel, lambda_init=lambda_init, tk=tk),
        grid=(n_heads, s_len // tq),
        in_specs=[pl.BlockSpec((tq, hw), lambda h, i: (i, h)),
                  pl.BlockSpec((s_len, hw), lambda h, i: (0, n_heads + h)),
                  pl.BlockSpec((s_len, hw), lambda h, i: (0, 2 * n_heads + h)),
                  vec, vec, vec, vec,
                  pl.BlockSpec((1, hw), lambda h, i: (0, 0))],
        out_specs=pl.BlockSpec((tq, hw), lambda h, i: (i, h)),
        out_shape=jax.ShapeDtypeStruct((s_len, d), BF16),
        scratch_shapes=[pltpu.VMEM((2, tq, HEAD_DIM), F32), pltpu.VMEM((2, tq, HEAD_DIM), F32),
                        pltpu.VMEM((2, tq, hw), F32)],
        compiler_params=_params(("parallel", "parallel"), vmem),
        name="diff_attn",
    )(qkv, qkv, qkv, lq1, lk1, lq2, lk2, g)


def _rope_tables(positions):
    inv = ROPE_THETA ** (-jnp.arange(0, ROT_DIM, 2, dtype=F32) / ROT_DIM)
    ang = positions.astype(F32)[:, None] * inv
    cos, sin = jnp.cos(ang), jnp.sin(ang)
    n = positions.shape[0]
    rest = HEAD_DIM - ROT_DIM
    c = jnp.concatenate([cos, cos, jnp.ones((n, rest), F32)], axis=1)
    s1 = jnp.concatenate([jnp.zeros((n, ROT_HALF), F32), sin, jnp.zeros((n, rest), F32)], axis=1)
    s2 = jnp.concatenate([-sin, jnp.zeros((n, ROT_HALF + rest), F32)], axis=1)
    return c, s1, s2


def kernel(x, positions, ev_w_in, ev_conv_w, ev_conv_b, ev_gate_a_w, ev_gate_a_b, ev_gate_x_w, ev_gate_x_b, ev_rg_lambda, ev_w_out, od_w_in, od_lambda_q1, od_lambda_k1, od_lambda_q2, od_lambda_k2, od_subln_g, od_w_out, ln_mix_g, ln_mix_b, ln_mlp_g, ln_mlp_b, mlp_w1, mlp_w2):
    batch, s_len, d = x.shape
    depth = ln_mix_g.shape[0]
    d_rnn = ev_conv_w.shape[-1]
    d_dil = (ev_w_in.shape[-1] - 2 * d_rnn) // 3
    alpha = (2 * depth) ** 0.25
    q_scale = LOG2E * HEAD_DIM ** -0.5
    row = lambda v: v.reshape(1, -1)

    outs = []
    for bi in range(batch):
        xf = x[bi]
        xb = xf.astype(BF16)
        tabs = _rope_tables(positions[bi])
        for layer in range(depth):
            p = layer // 2
            if layer % 2 == 0:
                w_in = ev_w_in[p].astype(BF16)
                ga = _proj(xb, w_in[:, :2 * d_rnn], F32)
                qkv = _proj_rope(xb, w_in[:, 2 * d_rnn:], tabs, d_dil, 2 * d_dil, q_scale)
                wa = _block_diag_groups(ev_gate_a_w[p], V7X_MXU_DIM).astype(BF16)
                wx = _block_diag_groups(ev_gate_x_w[p], V7X_MXU_DIM).astype(BF16)
                ya = _rglru(ga, ev_conv_w[p], row(ev_conv_b[p]), wa, wx, row(ev_gate_a_b[p]),
                            row(ev_gate_x_b[p]), row(ev_rg_lambda[p]))
                yb = _dilated_attention(qkv, 3)
                w_out = ev_w_out[p].astype(BF16)
                ys, ws = [ya, yb], [w_out[:d_rnn], w_out[d_rnn:]]
            else:
                lambda_init = 0.8 - 0.6 * math.exp(-0.3 * layer)
                qkv = _proj_rope(xb, od_w_in[p].astype(BF16), tabs, d, 2 * d, q_scale)
                y = _diff_attention(qkv, row(od_lambda_q1[p]), row(od_lambda_k1[p]), row(od_lambda_q2[p]),
                                    row(od_lambda_k2[p]), row(od_subln_g[p]), lambda_init)
                ys, ws = [y], [od_w_out[p].astype(BF16)]
            xf, xb = _outproj_ln(ys, ws, xf, row(ln_mix_g[layer]), row(ln_mix_b[layer]), alpha)
            xf, xb = _mlp_ln(xb, mlp_w1[layer].astype(BF16), mlp_w2[layer].astype(BF16), xf,
                             row(ln_mlp_g[layer]), row(ln_mlp_b[layer]), alpha)
        outs.append(xf)
    return outs[0][None] if batch == 1 else jnp.stack(outs, axis=0)
```

```python
import functools
import math

import jax
import jax.numpy as jnp
from jax import lax
from jax.experimental import pallas as pl
from jax.experimental.pallas import tpu as pltpu

HEAD_DIM = 128
ROT_DIM = HEAD_DIM // 4
ROT_HALF = ROT_DIM // 2
ROPE_THETA = 500000.0
CONV_W = 4
RG_C = 8.0
RNN_BLOCK_W = 64
DIL_PATTERNS = ((128, 1), (512, 4), (2048, 16))
BLK = 128
LN_EPS = 1e-5
RMS_EPS = 1e-5
NEG = -1e30
LOG2E = 1.4426950408889634

V7X_VMEM_BYTES = 64 * 2**20
V7X_VMEM_RESERVED_BYTES = 8 * 2**20
V7X_SUBLANES = 8
V7X_MXU_DIM = 256

F32 = jnp.float32
BF16 = jnp.bfloat16


def _params(semantics, vmem_bytes):
    limit = min(int(vmem_bytes), V7X_VMEM_BYTES - V7X_VMEM_RESERVED_BYTES)
    return pltpu.CompilerParams(dimension_semantics=semantics, vmem_limit_bytes=limit)


def _nbytes(shape, dtype):
    return math.prod(shape) * jnp.dtype(dtype).itemsize


def _ln_rows(z, g, b):
    mu = jnp.mean(z, axis=-1, keepdims=True)
    zc = z - mu
    var = jnp.mean(zc * zc, axis=-1, keepdims=True)
    return zc * lax.rsqrt(var + LN_EPS) * g + b


def _proj_kernel(x_ref, w_ref, o_ref):
    o_ref[...] = jnp.dot(x_ref[...], w_ref[...], preferred_element_type=F32).astype(o_ref.dtype)


def _proj_rope_kernel(x_ref, w_ref, c_ref, s1_ref, s2_ref, o_ref):
    acc = jnp.dot(x_ref[...], w_ref[...], preferred_element_type=F32)
    c, s1, s2 = c_ref[...], s1_ref[...], s2_ref[...]
    for h in range(acc.shape[1] // HEAD_DIM):
        cols = slice(h * HEAD_DIM, (h + 1) * HEAD_DIM)
        a = acc[:, cols]
        r = a * c + pltpu.roll(a, ROT_HALF, 1) * s1 + pltpu.roll(a, HEAD_DIM - ROT_HALF, 1) * s2
        o_ref[:, cols] = r.astype(o_ref.dtype)


def _proj_tiles(m, n):
    tm = min(1024, m)
    tn = min(1024, n)
    assert m % tm == 0 and n % tn == 0
    return tm, tn


def _proj(xb, w, out_dtype):
    m, k = xb.shape
    n = w.shape[1]
    tm, tn = _proj_tiles(m, n)
    vmem = 2 * (_nbytes((tm, k), BF16) + _nbytes((k, tn), BF16) + _nbytes((tm, tn), out_dtype)) \
        + 2 * _nbytes((tm, tn), F32)
    return pl.pallas_call(
        _proj_kernel,
        grid=(m // tm, n // tn),
        in_specs=[pl.BlockSpec((tm, k), lambda i, j: (i, 0)),
                  pl.BlockSpec((k, tn), lambda i, j: (0, j))],
        out_specs=pl.BlockSpec((tm, tn), lambda i, j: (i, j)),
        out_shape=jax.ShapeDtypeStruct((m, n), out_dtype),
        compiler_params=_params(("parallel", "arbitrary"), vmem),
        name="proj",
    )(xb, w)


def _proj_rope(xb, w, rope_tabs, n_q_cols, n_rope_cols):
    m, k = xb.shape
    n = w.shape[1]
    tm, tn = _proj_tiles(m, n)
    assert n_q_cols % tn == 0 and n_rope_cols % tn == 0
    n_q, n_rope = n_q_cols // tn, n_rope_cols // tn
    variant = lambda j: jnp.where(j < n_q, 0, jnp.where(j < n_rope, 1, 2))
    tab_spec = pl.BlockSpec((None, tm, HEAD_DIM), lambda i, j: (variant(j), i, 0))
    vmem = 2 * (_nbytes((tm, k), BF16) + _nbytes((k, tn), BF16) + _nbytes((tm, tn), BF16)
                + 3 * _nbytes((tm, HEAD_DIM), F32)) + 3 * _nbytes((tm, tn), F32)
    return pl.pallas_call(
        _proj_rope_kernel,
        grid=(m // tm, n // tn),
        in_specs=[pl.BlockSpec((tm, k), lambda i, j: (i, 0)),
                  pl.BlockSpec((k, tn), lambda i, j: (0, j)),
                  tab_spec, tab_spec, tab_spec],
        out_specs=pl.BlockSpec((tm, tn), lambda i, j: (i, j)),
        out_shape=jax.ShapeDtypeStruct((m, n), BF16),
        compiler_params=_params(("parallel", "arbitrary"), vmem),
        name="proj_rope",
    )(xb, w, *rope_tabs)


def _outproj_ln_kernel(*refs, n_in, alpha, sub):
    ys, ws = refs[:n_in], refs[n_in:2 * n_in]
    x_ref, g_ref, b_ref, of_ref, ob_ref = refs[2 * n_in:]
    for r in range(x_ref.shape[0] // sub):
        rows = slice(r * sub, (r + 1) * sub)
        acc = jnp.dot(ys[0][rows, :], ws[0][...], preferred_element_type=F32)
        for y_ref, w_ref in zip(ys[1:], ws[1:]):
            acc = acc + jnp.dot(y_ref[rows, :], w_ref[...], preferred_element_type=F32)
        y = _ln_rows(alpha * x_ref[rows, :] + acc, g_ref[...], b_ref[...])
        of_ref[rows, :] = y
        ob_ref[rows, :] = y.astype(BF16)


def _outproj_ln(ys, ws, xf, g, b, alpha):
    m, d = xf.shape
    tm = min(512, m)
    assert m % tm == 0
    row = lambda i: (i, 0)
    const = lambda i: (0, 0)
    in_specs = [pl.BlockSpec((tm, y.shape[1]), row) for y in ys]
    in_specs += [pl.BlockSpec(w.shape, const) for w in ws]
    in_specs += [pl.BlockSpec((tm, d), row), pl.BlockSpec((1, d), const), pl.BlockSpec((1, d), const)]
    vmem = 2 * (sum(_nbytes((tm, y.shape[1]), BF16) for y in ys) + sum(_nbytes(w.shape, BF16) for w in ws)
                + 2 * _nbytes((tm, d), F32) + _nbytes((tm, d), BF16)) + 3 * _nbytes((tm, d), F32)
    return pl.pallas_call(
        functools.partial(_outproj_ln_kernel, n_in=len(ys), alpha=alpha, sub=min(V7X_MXU_DIM, tm)),
        grid=(m // tm,),
        in_specs=in_specs,
        out_specs=[pl.BlockSpec((tm, d), row), pl.BlockSpec((tm, d), row)],
        out_shape=[jax.ShapeDtypeStruct((m, d), F32), jax.ShapeDtypeStruct((m, d), BF16)],
        compiler_params=_params(("parallel",), vmem),
        name="outproj_ln",
    )(*ys, *ws, xf, g, b)


def _mlp_ln_kernel(xb_ref, w1_ref, w2_ref, x_ref, g_ref, b_ref, of_ref, ob_ref, acc_ref, *, alpha):
    j = pl.program_id(1)

    @pl.when(j == 0)
    def _():
        acc_ref[...] = jnp.zeros_like(acc_ref)

    h = jnp.maximum(jnp.dot(xb_ref[...], w1_ref[...], preferred_element_type=F32), 0.0)
    acc_ref[...] += jnp.dot((h * h).astype(BF16), w2_ref[...], preferred_element_type=F32)

    @pl.when(j == pl.num_programs(1) - 1)
    def _():
        y = _ln_rows(alpha * x_ref[...] + acc_ref[...], g_ref[...], b_ref[...])
        of_ref[...] = y
        ob_ref[...] = y.astype(BF16)


def _mlp_ln(xb, w1, w2, xf, g, b, alpha):
    m, d = xf.shape
    f = w1.shape[1]
    tm = min(512, m)
    tf = min(1024, f)
    assert m % tm == 0 and f % tf == 0
    row = lambda i, j: (i, 0)
    const = lambda i, j: (0, 0)
    vmem = 2 * (_nbytes((tm, d), BF16) + 2 * _nbytes((d, tf), BF16) + 2 * _nbytes((tm, d), F32)
                + _nbytes((tm, d), BF16)) + 2 * _nbytes((tm, d), F32) + 2 * _nbytes((tm, tf), F32)
    return pl.pallas_call(
        functools.partial(_mlp_ln_kernel, alpha=alpha),
        grid=(m // tm, f // tf),
        in_specs=[pl.BlockSpec((tm, d), row),
                  pl.BlockSpec((d, tf), lambda i, j: (0, j)),
                  pl.BlockSpec((tf, d), lambda i, j: (j, 0)),
                  pl.BlockSpec((tm, d), row),
                  pl.BlockSpec((1, d), const), pl.BlockSpec((1, d), const)],
        out_specs=[pl.BlockSpec((tm, d), row), pl.BlockSpec((tm, d), row)],
        out_shape=[jax.ShapeDtypeStruct((m, d), F32), jax.ShapeDtypeStruct((m, d), BF16)],
        scratch_shapes=[pltpu.VMEM((tm, d), F32)],
        compiler_params=_params(("parallel", "arbitrary"), vmem),
        name="mlp_ln",
    )(xb, w1, w2, xf, g, b)


def _rglru_kernel(ag_ref, ax_ref, cw_ref, cb_ref, wa_ref, wx_ref, ba_ref, bx_ref, lam_ref,
                  o_ref, xs_ref, a_ref, b_ref, h_ref):
    i = pl.program_id(0)
    tm, c = ax_ref.shape
    sub = V7X_SUBLANES
    gw = wa_ref.shape[1]

    @pl.when(i == 0)
    def _():
        xs_ref[0:sub, :] = jnp.zeros((sub, c), F32)
        h_ref[...] = jnp.zeros_like(h_ref)

    x = ax_ref[...]
    xs_ref[sub:sub + tm, :] = x
    xc = cb_ref[...] + cw_ref[CONV_W - 1:CONV_W, :] * x
    for k in range(1, CONV_W):
        xc = xc + cw_ref[CONV_W - 1 - k:CONV_W - k, :] * xs_ref[pl.ds(sub - k, tm), :]
    xs_ref[0:sub, :] = x[tm - sub:tm, :]

    xcb = xc.astype(BF16)
    row_in_tile = lax.broadcasted_iota(jnp.int32, (tm, gw), 0) % sub
    for g in range(c // gw):
        cols = slice(g * gw, (g + 1) * gw)
        r = jax.nn.sigmoid(jnp.dot(xcb[:, cols], wa_ref[g], preferred_element_type=F32) + ba_ref[:, cols])
        ig = jax.nn.sigmoid(jnp.dot(xcb[:, cols], wx_ref[g], preferred_element_type=F32) + bx_ref[:, cols])
        log_a = (-RG_C * r) * jax.nn.softplus(-lam_ref[:, cols])
        a = jnp.exp(log_a)
        u = jnp.sqrt(-jnp.tanh(log_a) * (a * a + 1.0)) * (ig * xc[:, cols])
        for s in (1, 2, 4):
            keep = row_in_tile >= s
            u = jnp.where(keep, a * pltpu.roll(u, s, 0) + u, u)
            a = jnp.where(keep, a * pltpu.roll(a, s, 0), a)
        a_ref[:, cols] = a
        b_ref[:, cols] = u

    def tile_body(t, h):
        rows = pl.ds(pl.multiple_of(t * sub, sub), sub)
        hr = a_ref[rows, :] * h + b_ref[rows, :]
        b_ref[rows, :] = hr
        return jnp.broadcast_to(hr[sub - 1:sub, :], hr.shape)

    h_ref[...] = lax.fori_loop(0, tm // sub, tile_body, h_ref[...], unroll=4)

    gate = ag_ref[...]
    cdf = 0.5 * (1.0 + jnp.tanh(math.sqrt(2.0 / math.pi) * (gate + 0.044715 * (gate * gate * gate))))
    o_ref[...] = (gate * cdf * b_ref[...]).astype(o_ref.dtype)


def _rglru(ga, conv_w, conv_b, wa, wx, ba, bx, lam):
    m = ga.shape[0]
    c = ga.shape[1] // 2
    tm = min(256, m)
    assert m % tm == 0
    gw = wa.shape[1]
    const2 = lambda i: (0, 0)
    vmem = 2 * (2 * _nbytes((tm, c), F32) + _nbytes((tm, c), BF16) + 2 * _nbytes(wa.shape, BF16)) \
        + 3 * _nbytes((tm + 8, c), F32) + 12 * _nbytes((tm, c), F32)
    return pl.pallas_call(
        _rglru_kernel,
        grid=(m // tm,),
        in_specs=[pl.BlockSpec((tm, c), lambda i: (i, 0)),
                  pl.BlockSpec((tm, c), lambda i: (i, 1)),
                  pl.BlockSpec((CONV_W, c), const2), pl.BlockSpec((1, c), const2),
                  pl.BlockSpec(wa.shape, lambda i: (0, 0, 0)), pl.BlockSpec(wx.shape, lambda i: (0, 0, 0)),
                  pl.BlockSpec((1, c), const2), pl.BlockSpec((1, c), const2), pl.BlockSpec((1, c), const2)],
        out_specs=pl.BlockSpec((tm, c), lambda i: (i, 0)),
        out_shape=jax.ShapeDtypeStruct((m, c), BF16),
        scratch_shapes=[pltpu.VMEM((tm + V7X_SUBLANES, c), F32), pltpu.VMEM((tm, c), F32),
                        pltpu.VMEM((tm, c), F32), pltpu.VMEM((V7X_SUBLANES, c), F32)],
        compiler_params=_params(("arbitrary",), vmem),
        name="rglru",
    )(ga, ga, conv_w, conv_b, wa, wx, ba, bx, lam)


def _block_diag_groups(w, group):
    n, bw, _ = w.shape
    per = group // bw
    eye = jnp.eye(per, dtype=w.dtype)
    wg = w.reshape(n // per, per, bw, bw)
    return jnp.einsum('gpab,pq->gpaqb', wg, eye).reshape(n // per, group, group)


def _dil_kernel(*refs, has_prev, is_last):
    q_ref, kc_ref, kp_ref, vc_ref, vp_ref = refs[:5]
    refs = refs[5:]
    if has_prev:
        op_ref, lp_ref = refs[:2]
        refs = refs[2:]
    o_ref = refs[0]
    l_ref = None if is_last else refs[1]

    i = pl.program_id(1)
    tq = q_ref.shape[0]
    qi = lax.broadcasted_iota(jnp.int32, (BLK, 2 * BLK), 0)
    kj = lax.broadcasted_iota(jnp.int32, (BLK, 2 * BLK), 1)
    band = (kj >= qi) & (kj <= qi + BLK)
    contract_last = (((1,), (1,)), ((), ()))
    for b in range(tq // BLK):
        rows = slice(b * BLK, (b + 1) * BLK)
        valid = band & (kj >= jnp.where(i > 0, 0, BLK)) if b == 0 else band
        bias = jnp.where(valid, 0.0, NEG)
        for h in range(q_ref.shape[1] // HEAD_DIM):
            cols = slice(h * HEAD_DIM, (h + 1) * HEAD_DIM)
            if b == 0:
                kk = jnp.concatenate([kp_ref[:, cols], kc_ref[0:BLK, cols]], axis=0)
                vv = jnp.concatenate([vp_ref[:, cols], vc_ref[0:BLK, cols]], axis=0)
            else:
                kk = kc_ref[(b - 1) * BLK:(b + 1) * BLK, cols]
                vv = vc_ref[(b - 1) * BLK:(b + 1) * BLK, cols]
            s = lax.dot_general(q_ref[rows, cols], kk, contract_last, preferred_element_type=F32) + bias
            m = jnp.max(s, axis=-1, keepdims=True)
            p = jnp.exp2(s - m)
            l = jnp.sum(p, axis=-1, keepdims=True)
            o = jnp.dot(p.astype(BF16), vv, preferred_element_type=F32) / l
            lse = jnp.broadcast_to(m + jnp.log2(l), (BLK, HEAD_DIM))
            if has_prev:
                lp = lp_ref[rows, cols]
                mx = jnp.maximum(lp, lse)
                wa = jnp.exp2(lp - mx)
                wb = jnp.exp2(lse - mx)
                tot = wa + wb
                o = (op_ref[rows, cols] * wa + o * wb) / tot
                lse = mx + jnp.log2(tot)
            o_ref[rows, cols] = o.astype(o_ref.dtype)
            if not is_last:
                l_ref[rows, cols] = lse


def _dilated_attention(qkv, col_blocks):
    s_len = qkv.shape[0]
    c = qkv.shape[1] // 3
    assert col_blocks == 3
    o = lse = None
    for idx, (window, dil) in enumerate(DIL_PATTERNS):
        assert window // dil == BLK
        has_prev, is_last = idx > 0, idx == len(DIL_PATTERNS) - 1
        length = s_len // dil
        tq = min(512, length)
        assert length % tq == 0 and tq % BLK == 0
        per = tq // BLK
        qkv_v = qkv.reshape(length, dil * 3 * c)
        cur = lambda off: pl.BlockSpec((tq, c), lambda r, i, off=off: (i, 3 * r + off))
        prev = lambda off: pl.BlockSpec((BLK, c), lambda r, i, off=off: (jnp.maximum(i * per - 1, 0), 3 * r + off))
        nat = pl.BlockSpec((tq, c), lambda r, i: (i, r))
        in_specs = [cur(0), cur(1), prev(1), cur(2), prev(2)]
        args = [qkv_v] * 5
        if has_prev:
            in_specs += [nat, nat]
            args += [o.reshape(length, dil * c), lse.reshape(length, dil * c)]
        out_dtype = BF16 if is_last else F32
        out_specs = [nat] if is_last else [nat, nat]
        out_shape = [jax.ShapeDtypeStruct((length, dil * c), out_dtype)]
        if not is_last:
            out_shape.append(jax.ShapeDtypeStruct((length, dil * c), F32))
        vmem = 2 * (3 * _nbytes((tq, c), BF16) + 2 * _nbytes((BLK, c), BF16) + 4 * _nbytes((tq, c), F32)) \
            + 16 * _nbytes((BLK, 2 * BLK), F32) * (c // HEAD_DIM)
        res = pl.pallas_call(
            functools.partial(_dil_kernel, has_prev=has_prev, is_last=is_last),
            grid=(dil, length // tq),
            in_specs=in_specs,
            out_specs=out_specs,
            out_shape=out_shape,
            compiler_params=_params(("parallel", "parallel"), vmem),
            name=f"dilated_attn_d{dil}",
        )(*args)
        o = res[0].reshape(s_len, c)
        lse = None if is_last else res[1].reshape(s_len, c)
    return o


def _diff_attn_kernel(q_ref, k_ref, v_ref, lq1_ref, lk1_ref, lq2_ref, lk2_ref, g_ref, o_ref,
                      m_ref, l_ref, acc_ref, s_ref, smax_ref, *, lambda_init, tk):
    tq = q_ref.shape[0]
    assert tk % tq == 0
    q0 = pl.program_id(1) * tq
    jd = q0 // tk
    reps = tk // HEAD_DIM
    m_ref[...] = jnp.full_like(m_ref, NEG)
    l_ref[...] = jnp.zeros_like(l_ref)
    acc_ref[...] = jnp.zeros_like(acc_ref)
    contract_last = (((1,), (1,)), ((), ()))

    def kv_rows(j):
        return pl.ds(pl.multiple_of(j * tk, tk), tk)

    def scores(j, slot):
        kb = k_ref[kv_rows(j), :]
        for c in range(2):
            cols = slice(c * HEAD_DIM, (c + 1) * HEAD_DIM)
            s = lax.dot_general(q_ref[:, cols], kb[:, cols], contract_last, preferred_element_type=F32)
            s_ref[slot, c] = s
            smax_ref[slot, c] = jnp.broadcast_to(jnp.max(s, axis=-1, keepdims=True), (tq, HEAD_DIM))

    def update(j, slot, masked):
        vb = v_ref[kv_rows(j), :]
        for c in range(2):
            s = s_ref[slot, c]
            if masked:
                qi = lax.broadcasted_iota(jnp.int32, (tq, tk), 0)
                kj = lax.broadcasted_iota(jnp.int32, (tq, tk), 1)
                s = jnp.where(kj <= qi + (q0 - jd * tk), s, NEG)
                m_cur = jnp.max(s, axis=-1, keepdims=True)
            else:
                m_cur = smax_ref[slot, c]
            m_prev = m_ref[c]
            m_new = jnp.maximum(m_prev, m_cur)
            alpha = jnp.exp2(m_prev - m_new)
            p = jnp.exp2(s - jnp.concatenate([m_new] * reps, axis=1))
            l_ref[c] = alpha * l_ref[c] + jnp.sum(p, axis=-1, keepdims=True)
            m_ref[c] = m_new
            pv = jnp.dot(p.astype(BF16), vb, preferred_element_type=F32)
            acc_ref[c] = acc_ref[c] * jnp.concatenate([alpha, alpha], axis=1) + pv

    scores(0, 0)

    def pair(p, carry):
        scores(2 * p + 1, 1)
        update(2 * p, 0, False)
        scores(2 * p + 2, 0)
        update(2 * p + 1, 1, False)
        return carry

    lax.fori_loop(0, jd // 2, pair, 0)

    @pl.when(jd % 2 == 0)
    def _():
        update(jd, 0, True)

    @pl.when(jd % 2 == 1)
    def _():
        scores(jd, 1)
        update(jd - 1, 0, False)
        update(jd, 1, True)

    lam = (jnp.exp(jnp.sum(lq1_ref[...] * lk1_ref[...], axis=-1, keepdims=True))
           - jnp.exp(jnp.sum(lq2_ref[...] * lk2_ref[...], axis=-1, keepdims=True)) + lambda_init)
    inv1 = 1.0 / l_ref[0]
    inv2 = 1.0 / l_ref[1]
    o = acc_ref[0] * jnp.concatenate([inv1, inv1], axis=1) \
        - lam * (acc_ref[1] * jnp.concatenate([inv2, inv2], axis=1))
    o = o * lax.rsqrt(jnp.mean(o * o, axis=-1, keepdims=True) + RMS_EPS) * g_ref[...]
    o_ref[...] = (o * (1.0 - lambda_init)).astype(o_ref.dtype)


def _diff_attention(qkv, lq1, lk1, lq2, lk2, g, lambda_init):
    s_len = qkv.shape[0]
    d = qkv.shape[1] // 3
    hw = 2 * HEAD_DIM
    n_heads = d // hw
    tq = tk = min(512, s_len)
    assert s_len % tk == 0 and tk % tq == 0
    vec = pl.BlockSpec((1, HEAD_DIM), lambda h, i: (0, 0))
    kv_mode = pl.Buffered(1)
    vmem = 2 * _nbytes((s_len, hw), BF16) + 4 * _nbytes((tq, hw), BF16) \
        + 2 * (4 * _nbytes((tq, HEAD_DIM), F32) + _nbytes((tq, hw), F32)) + 16 * _nbytes((tq, tk), F32)
    return pl.pallas_call(
        functools.partial(_diff_attn_kernel, lambda_init=lambda_init, tk=tk),
        grid=(n_heads, s_len // tq),
        in_specs=[pl.BlockSpec((tq, hw), lambda h, i: (i, h)),
                  pl.BlockSpec((s_len, hw), lambda h, i: (0, n_heads + h), pipeline_mode=kv_mode),
                  pl.BlockSpec((s_len, hw), lambda h, i: (0, 2 * n_heads + h), pipeline_mode=kv_mode),
                  vec, vec, vec, vec,
                  pl.BlockSpec((1, hw), lambda h, i: (0, 0))],
        out_specs=pl.BlockSpec((tq, hw), lambda h, i: (i, h)),
        out_shape=jax.ShapeDtypeStruct((s_len, d), BF16),
        scratch_shapes=[pltpu.VMEM((2, tq, HEAD_DIM), F32), pltpu.VMEM((2, tq, HEAD_DIM), F32),
                        pltpu.VMEM((2, tq, hw), F32), pltpu.VMEM((2, 2, tq, tk), F32),
                        pltpu.VMEM((2, 2, tq, HEAD_DIM), F32)],
        compiler_params=_params(("parallel", "parallel"), vmem),
        name="diff_attn",
    )(qkv, qkv, qkv, lq1, lk1, lq2, lk2, g)


def _rope_tables(positions, q_scale):
    inv = ROPE_THETA ** (-jnp.arange(0, ROT_DIM, 2, dtype=F32) / ROT_DIM)
    ang = positions.astype(F32)[:, None] * inv
    cos, sin = jnp.cos(ang), jnp.sin(ang)
    n = positions.shape[0]
    rest = HEAD_DIM - ROT_DIM
    c = jnp.concatenate([cos, cos, jnp.ones((n, rest), F32)], axis=1)
    s1 = jnp.concatenate([jnp.zeros((n, ROT_HALF), F32), sin, jnp.zeros((n, rest), F32)], axis=1)
    s2 = jnp.concatenate([-sin, jnp.zeros((n, ROT_HALF + rest), F32)], axis=1)
    zero = jnp.zeros_like(c)
    return (jnp.stack([c * q_scale, c, jnp.ones_like(c)]),
            jnp.stack([s1 * q_scale, s1, zero]),
            jnp.stack([s2 * q_scale, s2, zero]))


def kernel(x, positions, ev_w_in, ev_conv_w, ev_conv_b, ev_gate_a_w, ev_gate_a_b, ev_gate_x_w, ev_gate_x_b, ev_rg_lambda, ev_w_out, od_w_in, od_lambda_q1, od_lambda_k1, od_lambda_q2, od_lambda_k2, od_subln_g, od_w_out, ln_mix_g, ln_mix_b, ln_mlp_g, ln_mlp_b, mlp_w1, mlp_w2):
    batch, s_len, d = x.shape
    depth = ln_mix_g.shape[0]
    d_rnn = ev_conv_w.shape[-1]
    d_dil = (ev_w_in.shape[-1] - 2 * d_rnn) // 3
    alpha = (2 * depth) ** 0.25
    q_scale = LOG2E * HEAD_DIM ** -0.5
    row = lambda v: v.reshape(1, -1)

    outs = []
    for bi in range(batch):
        xf = x[bi]
        xb = xf.astype(BF16)
        tabs = _rope_tables(positions[bi], q_scale)
        for layer in range(depth):
            p = layer // 2
            if layer % 2 == 0:
                w_in = ev_w_in[p].astype(BF16)
                ga = _proj(xb, w_in[:, :2 * d_rnn], F32)
                qkv = _proj_rope(xb, w_in[:, 2 * d_rnn:], tabs, d_dil, 2 * d_dil)
                wa = _block_diag_groups(ev_gate_a_w[p], V7X_MXU_DIM).astype(BF16)
                wx = _block_diag_groups(ev_gate_x_w[p], V7X_MXU_DIM).astype(BF16)
                ya = _rglru(ga, ev_conv_w[p], row(ev_conv_b[p]), wa, wx, row(ev_gate_a_b[p]),
                            row(ev_gate_x_b[p]), row(ev_rg_lambda[p]))
                yb = _dilated_attention(qkv, 3)
                w_out = ev_w_out[p].astype(BF16)
                ys, ws = [ya, yb], [w_out[:d_rnn], w_out[d_rnn:]]
            else:
                lambda_init = 0.8 - 0.6 * math.exp(-0.3 * layer)
                qkv = _proj_rope(xb, od_w_in[p].astype(BF16), tabs, d, 2 * d)
                y = _diff_attention(qkv, row(od_lambda_q1[p]), row(od_lambda_k1[p]), row(od_lambda_q2[p]),
                                    row(od_lambda_k2[p]), row(od_subln_g[p]), lambda_init)
                ys, ws = [y], [od_w_out[p].astype(BF16)]
            xf, xb = _outproj_ln(ys, ws, xf, row(ln_mix_g[layer]), row(ln_mix_b[layer]), alpha)
            xf, xb = _mlp_ln(xb, mlp_w1[layer].astype(BF16), mlp_w2[layer].astype(BF16), xf,
                             row(ln_mlp_g[layer]), row(ln_mlp_b[layer]), alpha)
        outs.append(xf)
    return outs[0][None] if batch == 1 else jnp.stack(outs, axis=0)
```

```python
import functools
import math

import jax
import jax.numpy as jnp
from jax import lax
from jax.experimental import pallas as pl
from jax.experimental.pallas import tpu as pltpu

HEAD_DIM = 128
ROT_DIM = HEAD_DIM // 4
ROT_HALF = ROT_DIM // 2
ROPE_THETA = 500000.0
CONV_W = 4
RG_C = 8.0
RNN_BLOCK_W = 64
DIL_PATTERNS = ((128, 1), (512, 4), (2048, 16))
BLK = 128
LN_EPS = 1e-5
RMS_EPS = 1e-5
NEG = -1e30
LOG2E = 1.4426950408889634

V7X_VMEM_BYTES = 64 * 2**20
V7X_VMEM_RESERVED_BYTES = 8 * 2**20
V7X_SUBLANES = 8
V7X_MXU_DIM = 256

F32 = jnp.float32
BF16 = jnp.bfloat16


def _params(semantics, vmem_bytes):
    limit = min(int(vmem_bytes), V7X_VMEM_BYTES - V7X_VMEM_RESERVED_BYTES)
    return pltpu.CompilerParams(dimension_semantics=semantics, vmem_limit_bytes=limit)


def _nbytes(shape, dtype):
    return math.prod(shape) * jnp.dtype(dtype).itemsize


def _ln_rows(z, g, b):
    mu = jnp.mean(z, axis=-1, keepdims=True)
    zc = z - mu
    var = jnp.mean(zc * zc, axis=-1, keepdims=True)
    return zc * lax.rsqrt(var + LN_EPS) * g + b


def _proj_kernel(x_ref, w_ref, o_ref):
    o_ref[...] = jnp.dot(x_ref[...], w_ref[...], preferred_element_type=F32).astype(o_ref.dtype)


def _proj_rope_kernel(x_ref, w_ref, c_ref, s1_ref, s2_ref, o_ref):
    acc = jnp.dot(x_ref[...], w_ref[...], preferred_element_type=F32)
    c, s1, s2 = c_ref[...], s1_ref[...], s2_ref[...]
    for h in range(acc.shape[1] // HEAD_DIM):
        cols = slice(h * HEAD_DIM, (h + 1) * HEAD_DIM)
        a = acc[:, cols]
        r = a * c + pltpu.roll(a, ROT_HALF, 1) * s1 + pltpu.roll(a, HEAD_DIM - ROT_HALF, 1) * s2
        o_ref[:, cols] = r.astype(o_ref.dtype)


def _proj_tiles(m, n):
    tm = min(1024, m)
    tn = min(1024, n)
    assert m % tm == 0 and n % tn == 0
    return tm, tn


def _proj(xb, w, out_dtype):
    m, k = xb.shape
    n = w.shape[1]
    tm, tn = _proj_tiles(m, n)
    vmem = 2 * (_nbytes((tm, k), BF16) + _nbytes((k, tn), BF16) + _nbytes((tm, tn), out_dtype)) \
        + 2 * _nbytes((tm, tn), F32)
    return pl.pallas_call(
        _proj_kernel,
        grid=(m // tm, n // tn),
        in_specs=[pl.BlockSpec((tm, k), lambda i, j: (i, 0)),
                  pl.BlockSpec((k, tn), lambda i, j: (0, j))],
        out_specs=pl.BlockSpec((tm, tn), lambda i, j: (i, j)),
        out_shape=jax.ShapeDtypeStruct((m, n), out_dtype),
        compiler_params=_params(("parallel", "arbitrary"), vmem),
        name="proj",
    )(xb, w)


def _proj_rope(xb, w, rope_tabs, n_q_cols, n_rope_cols):
    m, k = xb.shape
    n = w.shape[1]
    tm, tn = _proj_tiles(m, n)
    assert n_q_cols % tn == 0 and n_rope_cols % tn == 0
    n_q, n_rope = n_q_cols // tn, n_rope_cols // tn
    variant = lambda j: jnp.where(j < n_q, 0, jnp.where(j < n_rope, 1, 2))
    tab_spec = pl.BlockSpec((None, tm, HEAD_DIM), lambda i, j: (variant(j), i, 0))
    vmem = 2 * (_nbytes((tm, k), BF16) + _nbytes((k, tn), BF16) + _nbytes((tm, tn), BF16)
                + 3 * _nbytes((tm, HEAD_DIM), F32)) + 3 * _nbytes((tm, tn), F32)
    return pl.pallas_call(
        _proj_rope_kernel,
        grid=(m // tm, n // tn),
        in_specs=[pl.BlockSpec((tm, k), lambda i, j: (i, 0)),
                  pl.BlockSpec((k, tn), lambda i, j: (0, j)),
                  tab_spec, tab_spec, tab_spec],
        out_specs=pl.BlockSpec((tm, tn), lambda i, j: (i, j)),
        out_shape=jax.ShapeDtypeStruct((m, n), BF16),
        compiler_params=_params(("parallel", "arbitrary"), vmem),
        name="proj_rope",
    )(xb, w, *rope_tabs)


def _outproj_ln_kernel(*refs, n_in, alpha, sub):
    ys, ws = refs[:n_in], refs[n_in:2 * n_in]
    x_ref, g_ref, b_ref, of_ref, ob_ref = refs[2 * n_in:]
    for r in range(x_ref.shape[0] // sub):
        rows = slice(r * sub, (r + 1) * sub)
        acc = jnp.dot(ys[0][rows, :], ws[0][...], preferred_element_type=F32)
        for y_ref, w_ref in zip(ys[1:], ws[1:]):
            acc = acc + jnp.dot(y_ref[rows, :], w_ref[...], preferred_element_type=F32)
        y = _ln_rows(alpha * x_ref[rows, :] + acc, g_ref[...], b_ref[...])
        of_ref[rows, :] = y
        ob_ref[rows, :] = y.astype(BF16)


def _outproj_ln(ys, ws, xf, g, b, alpha):
    m, d = xf.shape
    tm = min(512, m)
    assert m % tm == 0
    row = lambda i: (i, 0)
    const = lambda i: (0, 0)
    in_specs = [pl.BlockSpec((tm, y.shape[1]), row) for y in ys]
    in_specs += [pl.BlockSpec(w.shape, const) for w in ws]
    in_specs += [pl.BlockSpec((tm, d), row), pl.BlockSpec((1, d), const), pl.BlockSpec((1, d), const)]
    vmem = 2 * (sum(_nbytes((tm, y.shape[1]), BF16) for y in ys) + sum(_nbytes(w.shape, BF16) for w in ws)
                + 2 * _nbytes((tm, d), F32) + _nbytes((tm, d), BF16)) + 3 * _nbytes((tm, d), F32)
    return pl.pallas_call(
        functools.partial(_outproj_ln_kernel, n_in=len(ys), alpha=alpha, sub=min(V7X_MXU_DIM, tm)),
        grid=(m // tm,),
        in_specs=in_specs,
        out_specs=[pl.BlockSpec((tm, d), row), pl.BlockSpec((tm, d), row)],
        out_shape=[jax.ShapeDtypeStruct((m, d), F32), jax.ShapeDtypeStruct((m, d), BF16)],
        compiler_params=_params(("parallel",), vmem),
        name="outproj_ln",
    )(*ys, *ws, xf, g, b)


def _mlp_ln_kernel(xb_ref, w1_ref, w2_ref, x_ref, g_ref, b_ref, of_ref, ob_ref, acc_ref, *, alpha):
    j = pl.program_id(1)

    @pl.when(j == 0)
    def _():
        acc_ref[...] = jnp.zeros_like(acc_ref)

    h = jnp.maximum(jnp.dot(xb_ref[...], w1_ref[...], preferred_element_type=F32), 0.0)
    acc_ref[...] += jnp.dot((h * h).astype(BF16), w2_ref[...], preferred_element_type=F32)

    @pl.when(j == pl.num_programs(1) - 1)
    def _():
        y = _ln_rows(alpha * x_ref[...] + acc_ref[...], g_ref[...], b_ref[...])
        of_ref[...] = y
        ob_ref[...] = y.astype(BF16)


def _mlp_ln(xb, w1, w2, xf, g, b, alpha):
    m, d = xf.shape
    f = w1.shape[1]
    tm = min(512, m)
    tf = min(1024, f)
    assert m % tm == 0 and f % tf == 0
    row = lambda i, j: (i, 0)
    const = lambda i, j: (0, 0)
    vmem = 2 * (_nbytes((tm, d), BF16) + 2 * _nbytes((d, tf), BF16) + 2 * _nbytes((tm, d), F32)
                + _nbytes((tm, d), BF16)) + 2 * _nbytes((tm, d), F32) + 2 * _nbytes((tm, tf), F32)
    return pl.pallas_call(
        functools.partial(_mlp_ln_kernel, alpha=alpha),
        grid=(m // tm, f // tf),
        in_specs=[pl.BlockSpec((tm, d), row),
                  pl.BlockSpec((d, tf), lambda i, j: (0, j)),
                  pl.BlockSpec((tf, d), lambda i, j: (j, 0)),
                  pl.BlockSpec((tm, d), row),
                  pl.BlockSpec((1, d), const), pl.BlockSpec((1, d), const)],
        out_specs=[pl.BlockSpec((tm, d), row), pl.BlockSpec((tm, d), row)],
        out_shape=[jax.ShapeDtypeStruct((m, d), F32), jax.ShapeDtypeStruct((m, d), BF16)],
        scratch_shapes=[pltpu.VMEM((tm, d), F32)],
        compiler_params=_params(("parallel", "arbitrary"), vmem),
        name="mlp_ln",
    )(xb, w1, w2, xf, g, b)


def _rglru_kernel(ag_ref, ax_ref, cw_ref, cb_ref, wa_ref, wx_ref, ba_ref, bx_ref, lam_ref,
                  o_ref, xs_ref, a_ref, b_ref, h_ref):
    i = pl.program_id(0)
    tm, c = ax_ref.shape
    sub = V7X_SUBLANES
    gw = wa_ref.shape[1]

    @pl.when(i == 0)
    def _():
        xs_ref[0:sub, :] = jnp.zeros((sub, c), F32)
        h_ref[...] = jnp.zeros_like(h_ref)

    x = ax_ref[...]
    xs_ref[sub:sub + tm, :] = x
    xc = cb_ref[...] + cw_ref[CONV_W - 1:CONV_W, :] * x
    for k in range(1, CONV_W):
        xc = xc + cw_ref[CONV_W - 1 - k:CONV_W - k, :] * xs_ref[pl.ds(sub - k, tm), :]
    xs_ref[0:sub, :] = x[tm - sub:tm, :]

    xcb = xc.astype(BF16)
    row_in_tile = lax.broadcasted_iota(jnp.int32, (tm, gw), 0) % sub
    for g in range(c // gw):
        cols = slice(g * gw, (g + 1) * gw)
        r = jax.nn.sigmoid(jnp.dot(xcb[:, cols], wa_ref[g], preferred_element_type=F32) + ba_ref[:, cols])
        ig = jax.nn.sigmoid(jnp.dot(xcb[:, cols], wx_ref[g], preferred_element_type=F32) + bx_ref[:, cols])
        log_a = (-RG_C * r) * jax.nn.softplus(-lam_ref[:, cols])
        a = jnp.exp(log_a)
        u = jnp.sqrt(-jnp.tanh(log_a) * (a * a + 1.0)) * (ig * xc[:, cols])
        for s in (1, 2, 4):
            keep = row_in_tile >= s
            u = jnp.where(keep, a * pltpu.roll(u, s, 0) + u, u)
            a = jnp.where(keep, a * pltpu.roll(a, s, 0), a)
        a_ref[:, cols] = a
        b_ref[:, cols] = u

    def tile_body(t, h):
        rows = pl.ds(pl.multiple_of(t * sub, sub), sub)
        hr = a_ref[rows, :] * h + b_ref[rows, :]
        b_ref[rows, :] = hr
        return jnp.broadcast_to(hr[sub - 1:sub, :], hr.shape)

    h_ref[...] = lax.fori_loop(0, tm // sub, tile_body, h_ref[...], unroll=4)

    gate = ag_ref[...]
    cdf = 0.5 * (1.0 + jnp.tanh(math.sqrt(2.0 / math.pi) * (gate + 0.044715 * (gate * gate * gate))))
    o_ref[...] = (gate * cdf * b_ref[...]).astype(o_ref.dtype)


def _rglru(ga, conv_w, conv_b, wa, wx, ba, bx, lam):
    m = ga.shape[0]
    c = ga.shape[1] // 2
    tm = min(256, m)
    assert m % tm == 0
    gw = wa.shape[1]
    const2 = lambda i: (0, 0)
    vmem = 2 * (2 * _nbytes((tm, c), F32) + _nbytes((tm, c), BF16) + 2 * _nbytes(wa.shape, BF16)) \
        + 3 * _nbytes((tm + 8, c), F32) + 12 * _nbytes((tm, c), F32)
    return pl.pallas_call(
        _rglru_kernel,
        grid=(m // tm,),
        in_specs=[pl.BlockSpec((tm, c), lambda i: (i, 0)),
                  pl.BlockSpec((tm, c), lambda i: (i, 1)),
                  pl.BlockSpec((CONV_W, c), const2), pl.BlockSpec((1, c), const2),
                  pl.BlockSpec(wa.shape, lambda i: (0, 0, 0)), pl.BlockSpec(wx.shape, lambda i: (0, 0, 0)),
                  pl.BlockSpec((1, c), const2), pl.BlockSpec((1, c), const2), pl.BlockSpec((1, c), const2)],
        out_specs=pl.BlockSpec((tm, c), lambda i: (i, 0)),
        out_shape=jax.ShapeDtypeStruct((m, c), BF16),
        scratch_shapes=[pltpu.VMEM((tm + V7X_SUBLANES, c), F32), pltpu.VMEM((tm, c), F32),
                        pltpu.VMEM((tm, c), F32), pltpu.VMEM((V7X_SUBLANES, c), F32)],
        compiler_params=_params(("arbitrary",), vmem),
        name="rglru",
    )(ga, ga, conv_w, conv_b, wa, wx, ba, bx, lam)


def _block_diag_groups(w, group):
    n, bw, _ = w.shape
    per = group // bw
    eye = jnp.eye(per, dtype=w.dtype)
    wg = w.reshape(n // per, per, bw, bw)
    return jnp.einsum('gpab,pq->gpaqb', wg, eye).reshape(n // per, group, group)


def _dil_kernel(*refs, has_prev, is_last):
    q_ref, kc_ref, kp_ref, vc_ref, vp_ref = refs[:5]
    refs = refs[5:]
    if has_prev:
        op_ref, lp_ref = refs[:2]
        refs = refs[2:]
    o_ref = refs[0]
    l_ref = None if is_last else refs[1]

    i = pl.program_id(1)
    tq = q_ref.shape[0]
    qi = lax.broadcasted_iota(jnp.int32, (BLK, 2 * BLK), 0)
    kj = lax.broadcasted_iota(jnp.int32, (BLK, 2 * BLK), 1)
    band = (kj >= qi) & (kj <= qi + BLK)
    contract_last = (((1,), (1,)), ((), ()))
    for b in range(tq // BLK):
        rows = slice(b * BLK, (b + 1) * BLK)
        valid = band & (kj >= jnp.where(i > 0, 0, BLK)) if b == 0 else band
        bias = jnp.where(valid, 0.0, NEG)
        for h in range(q_ref.shape[1] // HEAD_DIM):
            cols = slice(h * HEAD_DIM, (h + 1) * HEAD_DIM)
            if b == 0:
                kk = jnp.concatenate([kp_ref[:, cols], kc_ref[0:BLK, cols]], axis=0)
                vv = jnp.concatenate([vp_ref[:, cols], vc_ref[0:BLK, cols]], axis=0)
            else:
                kk = kc_ref[(b - 1) * BLK:(b + 1) * BLK, cols]
                vv = vc_ref[(b - 1) * BLK:(b + 1) * BLK, cols]
            s = lax.dot_general(q_ref[rows, cols], kk, contract_last, preferred_element_type=F32) + bias
            m = jnp.max(s, axis=-1, keepdims=True)
            p = jnp.exp2(s - m)
            l = jnp.sum(p, axis=-1, keepdims=True)
            o = jnp.dot(p.astype(BF16), vv, preferred_element_type=F32) / l
            lse = jnp.broadcast_to(m + jnp.log2(l), (BLK, HEAD_DIM))
            if has_prev:
                lp = lp_ref[rows, cols]
                mx = jnp.maximum(lp, lse)
                wa = jnp.exp2(lp - mx)
                wb = jnp.exp2(lse - mx)
                tot = wa + wb
                o = (op_ref[rows, cols] * wa + o * wb) / tot
                lse = mx + jnp.log2(tot)
            o_ref[rows, cols] = o.astype(o_ref.dtype)
            if not is_last:
                l_ref[rows, cols] = lse


def _dilated_attention(qkv, col_blocks):
    s_len = qkv.shape[0]
    c = qkv.shape[1] // 3
    assert col_blocks == 3
    o = lse = None
    for idx, (window, dil) in enumerate(DIL_PATTERNS):
        assert window // dil == BLK
        has_prev, is_last = idx > 0, idx == len(DIL_PATTERNS) - 1
        length = s_len // dil
        tq = min(512, length)
        assert length % tq == 0 and tq % BLK == 0
        per = tq // BLK
        qkv_v = qkv.reshape(length, dil * 3 * c)
        cur = lambda off: pl.BlockSpec((tq, c), lambda r, i, off=off: (i, 3 * r + off))
        prev = lambda off: pl.BlockSpec((BLK, c), lambda r, i, off=off: (jnp.maximum(i * per - 1, 0), 3 * r + off))
        nat = pl.BlockSpec((tq, c), lambda r, i: (i, r))
        in_specs = [cur(0), cur(1), prev(1), cur(2), prev(2)]
        args = [qkv_v] * 5
        if has_prev:
            in_specs += [nat, nat]
            args += [o.reshape(length, dil * c), lse.reshape(length, dil * c)]
        out_dtype = BF16 if is_last else F32
        out_specs = [nat] if is_last else [nat, nat]
        out_shape = [jax.ShapeDtypeStruct((length, dil * c), out_dtype)]
        if not is_last:
            out_shape.append(jax.ShapeDtypeStruct((length, dil * c), F32))
        vmem = 2 * (3 * _nbytes((tq, c), BF16) + 2 * _nbytes((BLK, c), BF16) + 4 * _nbytes((tq, c), F32)) \
            + 16 * _nbytes((BLK, 2 * BLK), F32) * (c // HEAD_DIM)
        res = pl.pallas_call(
            functools.partial(_dil_kernel, has_prev=has_prev, is_last=is_last),
            grid=(dil, length // tq),
            in_specs=in_specs,
            out_specs=out_specs,
            out_shape=out_shape,
            compiler_params=_params(("parallel", "parallel"), vmem),
            name=f"dilated_attn_d{dil}",
        )(*args)
        o = res[0].reshape(s_len, c)
        lse = None if is_last else res[1].reshape(s_len, c)
    return o


def _diff_attn_kernel(q_ref, k_ref, vt_ref, lq1_ref, lk1_ref, lq2_ref, lk2_ref, g_ref, o_ref,
                      m_ref, l_ref, acc_ref, s_ref, smax_ref, *, lambda_init, tk):
    tq = q_ref.shape[0]
    assert tk % tq == 0
    q0 = pl.program_id(1) * tq
    jd = q0 // tk
    m_ref[...] = jnp.full_like(m_ref, NEG)
    l_ref[...] = jnp.zeros_like(l_ref)
    acc_ref[...] = jnp.zeros_like(acc_ref)
    contract_last = (((1,), (1,)), ((), ()))

    def scores(j, slot):
        kb = k_ref[pl.ds(pl.multiple_of(j * tk, tk), tk), :]
        for c in range(2):
            cols = slice(c * HEAD_DIM, (c + 1) * HEAD_DIM)
            s = lax.dot_general(kb[:, cols], q_ref[:, cols], contract_last, preferred_element_type=F32)
            s_ref[slot, c] = s
            smax_ref[slot, c] = jnp.max(s, axis=0, keepdims=True)

    def update(j, slot, masked):
        vt = vt_ref[j]
        for c in range(2):
            s = s_ref[slot, c]
            if masked:
                kj = lax.broadcasted_iota(jnp.int32, (tk, tq), 0)
                qi = lax.broadcasted_iota(jnp.int32, (tk, tq), 1)
                s = jnp.where(kj <= qi + (q0 - jd * tk), s, NEG)
                m_cur = jnp.max(s, axis=0, keepdims=True)
            else:
                m_cur = smax_ref[slot, c]
            m_prev = m_ref[c]
            m_new = jnp.maximum(m_prev, m_cur)
            alpha = jnp.exp2(m_prev - m_new)
            p = jnp.exp2(s - m_new)
            l_ref[c] = alpha * l_ref[c] + jnp.sum(p, axis=0, keepdims=True)
            m_ref[c] = m_new
            pv = jnp.dot(vt, p.astype(BF16), preferred_element_type=F32)
            acc_ref[c] = acc_ref[c] * alpha + pv

    scores(0, 0)

    def pair(p, carry):
        scores(2 * p + 1, 1)
        update(2 * p, 0, False)
        scores(2 * p + 2, 0)
        update(2 * p + 1, 1, False)
        return carry

    lax.fori_loop(0, jd // 2, pair, 0)

    @pl.when(jd % 2 == 0)
    def _():
        update(jd, 0, True)

    @pl.when(jd % 2 == 1)
    def _():
        scores(jd, 1)
        update(jd - 1, 0, False)
        update(jd, 1, True)

    lam = (jnp.exp(jnp.sum(lq1_ref[...] * lk1_ref[...], axis=-1, keepdims=True))
           - jnp.exp(jnp.sum(lq2_ref[...] * lk2_ref[...], axis=-1, keepdims=True)) + lambda_init)
    o = acc_ref[0] * (1.0 / l_ref[0]) - lam * (acc_ref[1] * (1.0 / l_ref[1]))
    g = jnp.concatenate([g_ref[...]] * (tq // HEAD_DIM), axis=1)
    o = o * lax.rsqrt(jnp.mean(o * o, axis=0, keepdims=True) + RMS_EPS) * g
    o_ref[...] = (o * (1.0 - lambda_init)).T.astype(o_ref.dtype)


def _diff_attention(qkv, lq1, lk1, lq2, lk2, g, lambda_init):
    s_len = qkv.shape[0]
    d = qkv.shape[1] // 3
    hw = 2 * HEAD_DIM
    n_heads = d // hw
    tq = tk = min(512, s_len)
    assert s_len % tk == 0 and tk % tq == 0
    n_kv = s_len // tk
    vt = qkv[:, 2 * d:].reshape(n_kv, tk, n_heads, hw).transpose(2, 0, 3, 1)
    g_cols = jnp.broadcast_to(g.reshape(hw, 1), (hw, HEAD_DIM))
    vec = pl.BlockSpec((1, HEAD_DIM), lambda h, i: (0, 0))
    kv_mode = pl.Buffered(1)
    vmem = 2 * _nbytes((s_len, hw), BF16) + 4 * _nbytes((tq, hw), BF16) \
        + 2 * (4 * _nbytes((V7X_SUBLANES, tq), F32) + _nbytes((tq, hw), F32)) + 16 * _nbytes((tq, tk), F32)
    return pl.pallas_call(
        functools.partial(_diff_attn_kernel, lambda_init=lambda_init, tk=tk),
        grid=(n_heads, s_len // tq),
        in_specs=[pl.BlockSpec((tq, hw), lambda h, i: (i, h)),
                  pl.BlockSpec((s_len, hw), lambda h, i: (0, n_heads + h), pipeline_mode=kv_mode),
                  pl.BlockSpec((None, n_kv, hw, tk), lambda h, i: (h, 0, 0, 0), pipeline_mode=kv_mode),
                  vec, vec, vec, vec,
                  pl.BlockSpec((hw, HEAD_DIM), lambda h, i: (0, 0))],
        out_specs=pl.BlockSpec((tq, hw), lambda h, i: (i, h)),
        out_shape=jax.ShapeDtypeStruct((s_len, d), BF16),
        scratch_shapes=[pltpu.VMEM((2, 1, tq), F32), pltpu.VMEM((2, 1, tq), F32),
                        pltpu.VMEM((2, hw, tq), F32), pltpu.VMEM((2, 2, tk, tq), F32),
                        pltpu.VMEM((2, 2, 1, tq), F32)],
        compiler_params=_params(("parallel", "parallel"), vmem),
        name="diff_attn",
    )(qkv, qkv, vt, lq1, lk1, lq2, lk2, g_cols)


def _rope_tables(positions, q_scale):
    inv = ROPE_THETA ** (-jnp.arange(0, ROT_DIM, 2, dtype=F32) / ROT_DIM)
    ang = positions.astype(F32)[:, None] * inv
    cos, sin = jnp.cos(ang), jnp.sin(ang)
    n = positions.shape[0]
    rest = HEAD_DIM - ROT_DIM
    c = jnp.concatenate([cos, cos, jnp.ones((n, rest), F32)], axis=1)
    s1 = jnp.concatenate([jnp.zeros((n, ROT_HALF), F32), sin, jnp.zeros((n, rest), F32)], axis=1)
    s2 = jnp.concatenate([-sin, jnp.zeros((n, ROT_HALF + rest), F32)], axis=1)
    zero = jnp.zeros_like(c)
    return (jnp.stack([c * q_scale, c, jnp.ones_like(c)]),
            jnp.stack([s1 * q_scale, s1, zero]),
            jnp.stack([s2 * q_scale, s2, zero]))


def kernel(x, positions, ev_w_in, ev_conv_w, ev_conv_b, ev_gate_a_w, ev_gate_a_b, ev_gate_x_w, ev_gate_x_b, ev_rg_lambda, ev_w_out, od_w_in, od_lambda_q1, od_lambda_k1, od_lambda_q2, od_lambda_k2, od_subln_g, od_w_out, ln_mix_g, ln_mix_b, ln_mlp_g, ln_mlp_b, mlp_w1, mlp_w2):
    batch, s_len, d = x.shape
    depth = ln_mix_g.shape[0]
    d_rnn = ev_conv_w.shape[-1]
    d_dil = (ev_w_in.shape[-1] - 2 * d_rnn) // 3
    alpha = (2 * depth) ** 0.25
    q_scale = LOG2E * HEAD_DIM ** -0.5
    row = lambda v: v.reshape(1, -1)

    outs = []
    for bi in range(batch):
        xf = x[bi]
        xb = xf.astype(BF16)
        tabs = _rope_tables(positions[bi], q_scale)
        for layer in range(depth):
            p = layer // 2
            if layer % 2 == 0:
                w_in = ev_w_in[p].astype(BF16)
                ga = _proj(xb, w_in[:, :2 * d_rnn], F32)
                qkv = _proj_rope(xb, w_in[:, 2 * d_rnn:], tabs, d_dil, 2 * d_dil)
                wa = _block_diag_groups(ev_gate_a_w[p], V7X_MXU_DIM).astype(BF16)
                wx = _block_diag_groups(ev_gate_x_w[p], V7X_MXU_DIM).astype(BF16)
                ya = _rglru(ga, ev_conv_w[p], row(ev_conv_b[p]), wa, wx, row(ev_gate_a_b[p]),
                            row(ev_gate_x_b[p]), row(ev_rg_lambda[p]))
                yb = _dilated_attention(qkv, 3)
                w_out = ev_w_out[p].astype(BF16)
                ys, ws = [ya, yb], [w_out[:d_rnn], w_out[d_rnn:]]
            else:
                lambda_init = 0.8 - 0.6 * math.exp(-0.3 * layer)
                qkv = _proj_rope(xb, od_w_in[p].astype(BF16), tabs, d, 2 * d)
                y = _diff_attention(qkv, row(od_lambda_q1[p]), row(od_lambda_k1[p]), row(od_lambda_q2[p]),
                                    row(od_lambda_k2[p]), row(od_subln_g[p]), lambda_init)
                ys, ws = [y], [od_w_out[p].astype(BF16)]
            xf, xb = _outproj_ln(ys, ws, xf, row(ln_mix_g[layer]), row(ln_mix_b[layer]), alpha)
            xf, xb = _mlp_ln(xb, mlp_w1[layer].astype(BF16), mlp_w2[layer].astype(BF16), xf,
                             row(ln_mlp_g[layer]), row(ln_mlp_b[layer]), alpha)
        outs.append(xf)
    return outs[0][None] if batch == 1 else jnp.stack(outs, axis=0)
```

```python
import functools
import math

import jax
import jax.numpy as jnp
from jax import lax
from jax.experimental import pallas as pl
from jax.experimental.pallas import tpu as pltpu

HEAD_DIM = 128
ROT_DIM = HEAD_DIM // 4
ROT_HALF = ROT_DIM // 2
ROPE_THETA = 500000.0
CONV_W = 4
RG_C = 8.0
RNN_BLOCK_W = 64
DIL_PATTERNS = ((128, 1), (512, 4), (2048, 16))
BLK = 128
LN_EPS = 1e-5
RMS_EPS = 1e-5
NEG = -1e30
LOG2E = 1.4426950408889634

V7X_VMEM_BYTES = 64 * 2**20
V7X_VMEM_RESERVED_BYTES = 8 * 2**20
V7X_SUBLANES = 8
V7X_MXU_DIM = 256

F32 = jnp.float32
BF16 = jnp.bfloat16


def _params(semantics, vmem_bytes):
    limit = min(int(vmem_bytes), V7X_VMEM_BYTES - V7X_VMEM_RESERVED_BYTES)
    return pltpu.CompilerParams(dimension_semantics=semantics, vmem_limit_bytes=limit)


def _nbytes(shape, dtype):
    return math.prod(shape) * jnp.dtype(dtype).itemsize


def _ln_rows(z, g, b):
    mu = jnp.mean(z, axis=-1, keepdims=True)
    zc = z - mu
    var = jnp.mean(zc * zc, axis=-1, keepdims=True)
    return zc * lax.rsqrt(var + LN_EPS) * g + b


def _proj_kernel(x_ref, w_ref, o_ref):
    o_ref[...] = jnp.dot(x_ref[...], w_ref[...], preferred_element_type=F32).astype(o_ref.dtype)


def _proj_rope_kernel(x_ref, w_ref, c_ref, s1_ref, s2_ref, o_ref, *rest, dilations):
    dil_refs, slab_ref = rest[:len(dilations)], (rest[-1] if dilations else None)
    tm = x_ref.shape[0]
    acc = jnp.dot(x_ref[...], w_ref[...], preferred_element_type=F32)
    c, s1, s2 = c_ref[...], s1_ref[...], s2_ref[...]
    for h in range(acc.shape[1] // HEAD_DIM):
        cols = slice(h * HEAD_DIM, (h + 1) * HEAD_DIM)
        a = acc[:, cols]
        r = a * c + pltpu.roll(a, ROT_HALF, 1) * s1 + pltpu.roll(a, HEAD_DIM - ROT_HALF, 1) * s2
        o_ref[:, cols] = r.astype(o_ref.dtype)
        if dilations:
            slab_ref[h * tm:(h + 1) * tm, :] = r
            for d, d_ref in zip(dilations, dil_refs):
                for rho in range(d):
                    rows = pl.ds(h * tm + rho, tm // d, stride=d)
                    d_ref[rho, :, cols] = slab_ref[rows, :].astype(d_ref.dtype)


def _proj_tiles(m, n):
    tm = min(1024, m)
    tn = min(1024, n)
    assert m % tm == 0 and n % tn == 0
    return tm, tn


def _proj(xb, w, out_dtype):
    m, k = xb.shape
    n = w.shape[1]
    tm, tn = _proj_tiles(m, n)
    vmem = 2 * (_nbytes((tm, k), BF16) + _nbytes((k, tn), BF16) + _nbytes((tm, tn), out_dtype)) \
        + 2 * _nbytes((tm, tn), F32)
    return pl.pallas_call(
        _proj_kernel,
        grid=(m // tm, n // tn),
        in_specs=[pl.BlockSpec((tm, k), lambda i, j: (i, 0)),
                  pl.BlockSpec((k, tn), lambda i, j: (0, j))],
        out_specs=pl.BlockSpec((tm, tn), lambda i, j: (i, j)),
        out_shape=jax.ShapeDtypeStruct((m, n), out_dtype),
        compiler_params=_params(("parallel", "arbitrary"), vmem),
        name="proj",
    )(xb, w)


def _proj_rope(xb, w, rope_tabs, n_q_cols, n_rope_cols, dilations=()):
    m, k = xb.shape
    n = w.shape[1]
    tm, tn = _proj_tiles(m, n)
    assert n_q_cols % tn == 0 and n_rope_cols % tn == 0
    n_q, n_rope = n_q_cols // tn, n_rope_cols // tn
    variant = lambda j: jnp.where(j < n_q, 0, jnp.where(j < n_rope, 1, 2))
    tab_spec = pl.BlockSpec((None, tm, HEAD_DIM), lambda i, j: (variant(j), i, 0))
    out_specs = [pl.BlockSpec((tm, tn), lambda i, j: (i, j))]
    out_shape = [jax.ShapeDtypeStruct((m, n), BF16)]
    for d in dilations:
        assert tm % (d * 2 * V7X_SUBLANES) == 0
        out_specs.append(pl.BlockSpec((d, tm // d, tn), lambda i, j: (0, i, j)))
        out_shape.append(jax.ShapeDtypeStruct((d, m // d, n), BF16))
    scratch = [pltpu.VMEM((tn // HEAD_DIM * tm, HEAD_DIM), F32)] if dilations else []
    vmem = 2 * (_nbytes((tm, k), BF16) + _nbytes((k, tn), BF16) + (1 + len(dilations)) * _nbytes((tm, tn), BF16)
                + 3 * _nbytes((tm, HEAD_DIM), F32)) + (3 + bool(dilations)) * _nbytes((tm, tn), F32)
    return pl.pallas_call(
        functools.partial(_proj_rope_kernel, dilations=tuple(dilations)),
        grid=(m // tm, n // tn),
        in_specs=[pl.BlockSpec((tm, k), lambda i, j: (i, 0)),
                  pl.BlockSpec((k, tn), lambda i, j: (0, j)),
                  tab_spec, tab_spec, tab_spec],
        out_specs=out_specs,
        out_shape=out_shape,
        scratch_shapes=scratch,
        compiler_params=_params(("parallel", "arbitrary"), vmem),
        name="proj_rope",
    )(xb, w, *rope_tabs)


def _outproj_ln_kernel(*refs, n_in, alpha, sub):
    ys, ws = refs[:n_in], refs[n_in:2 * n_in]
    x_ref, g_ref, b_ref, of_ref, ob_ref = refs[2 * n_in:]
    for r in range(x_ref.shape[0] // sub):
        rows = slice(r * sub, (r + 1) * sub)
        acc = jnp.dot(ys[0][rows, :], ws[0][...], preferred_element_type=F32)
        for y_ref, w_ref in zip(ys[1:], ws[1:]):
            acc = acc + jnp.dot(y_ref[rows, :], w_ref[...], preferred_element_type=F32)
        y = _ln_rows(alpha * x_ref[rows, :] + acc, g_ref[...], b_ref[...])
        of_ref[rows, :] = y
        ob_ref[rows, :] = y.astype(BF16)


def _outproj_ln(ys, ws, xf, g, b, alpha):
    m, d = xf.shape
    tm = min(512, m)
    assert m % tm == 0
    row = lambda i: (i, 0)
    const = lambda i: (0, 0)
    in_specs = [pl.BlockSpec((tm, y.shape[1]), row) for y in ys]
    in_specs += [pl.BlockSpec(w.shape, const) for w in ws]
    in_specs += [pl.BlockSpec((tm, d), row), pl.BlockSpec((1, d), const), pl.BlockSpec((1, d), const)]
    vmem = 2 * (sum(_nbytes((tm, y.shape[1]), BF16) for y in ys) + sum(_nbytes(w.shape, BF16) for w in ws)
                + 2 * _nbytes((tm, d), F32) + _nbytes((tm, d), BF16)) + 3 * _nbytes((tm, d), F32)
    return pl.pallas_call(
        functools.partial(_outproj_ln_kernel, n_in=len(ys), alpha=alpha, sub=min(V7X_MXU_DIM, tm)),
        grid=(m // tm,),
        in_specs=in_specs,
        out_specs=[pl.BlockSpec((tm, d), row), pl.BlockSpec((tm, d), row)],
        out_shape=[jax.ShapeDtypeStruct((m, d), F32), jax.ShapeDtypeStruct((m, d), BF16)],
        compiler_params=_params(("parallel",), vmem),
        name="outproj_ln",
    )(*ys, *ws, xf, g, b)


def _mlp_ln_kernel(xb_ref, w1_ref, w2_ref, x_ref, g_ref, b_ref, of_ref, ob_ref, acc_ref, *, alpha):
    j = pl.program_id(1)

    @pl.when(j == 0)
    def _():
        acc_ref[...] = jnp.zeros_like(acc_ref)

    h = jnp.maximum(jnp.dot(xb_ref[...], w1_ref[...], preferred_element_type=F32), 0.0)
    acc_ref[...] += jnp.dot((h * h).astype(BF16), w2_ref[...], preferred_element_type=F32)

    @pl.when(j == pl.num_programs(1) - 1)
    def _():
        y = _ln_rows(alpha * x_ref[...] + acc_ref[...], g_ref[...], b_ref[...])
        of_ref[...] = y
        ob_ref[...] = y.astype(BF16)


def _mlp_ln(xb, w1, w2, xf, g, b, alpha):
    m, d = xf.shape
    f = w1.shape[1]
    tm = min(512, m)
    tf = min(1024, f)
    assert m % tm == 0 and f % tf == 0
    row = lambda i, j: (i, 0)
    const = lambda i, j: (0, 0)
    vmem = 2 * (_nbytes((tm, d), BF16) + 2 * _nbytes((d, tf), BF16) + 2 * _nbytes((tm, d), F32)
                + _nbytes((tm, d), BF16)) + 2 * _nbytes((tm, d), F32) + 2 * _nbytes((tm, tf), F32)
    return pl.pallas_call(
        functools.partial(_mlp_ln_kernel, alpha=alpha),
        grid=(m // tm, f // tf),
        in_specs=[pl.BlockSpec((tm, d), row),
                  pl.BlockSpec((d, tf), lambda i, j: (0, j)),
                  pl.BlockSpec((tf, d), lambda i, j: (j, 0)),
                  pl.BlockSpec((tm, d), row),
                  pl.BlockSpec((1, d), const), pl.BlockSpec((1, d), const)],
        out_specs=[pl.BlockSpec((tm, d), row), pl.BlockSpec((tm, d), row)],
        out_shape=[jax.ShapeDtypeStruct((m, d), F32), jax.ShapeDtypeStruct((m, d), BF16)],
        scratch_shapes=[pltpu.VMEM((tm, d), F32)],
        compiler_params=_params(("parallel", "arbitrary"), vmem),
        name="mlp_ln",
    )(xb, w1, w2, xf, g, b)


def _rglru_kernel(ag_ref, ax_ref, cw_ref, cb_ref, wa_ref, wx_ref, ba_ref, bx_ref, lam_ref,
                  o_ref, xs_ref, a_ref, b_ref, h_ref):
    i = pl.program_id(0)
    tm, c = ax_ref.shape
    sub = V7X_SUBLANES
    gw = wa_ref.shape[1]

    @pl.when(i == 0)
    def _():
        xs_ref[0:sub, :] = jnp.zeros((sub, c), F32)
        h_ref[...] = jnp.zeros_like(h_ref)

    x = ax_ref[...]
    xs_ref[sub:sub + tm, :] = x
    xc = cb_ref[...] + cw_ref[CONV_W - 1:CONV_W, :] * x
    for k in range(1, CONV_W):
        xc = xc + cw_ref[CONV_W - 1 - k:CONV_W - k, :] * xs_ref[pl.ds(sub - k, tm), :]
    xs_ref[0:sub, :] = x[tm - sub:tm, :]

    xcb = xc.astype(BF16)
    row_in_tile = lax.broadcasted_iota(jnp.int32, (tm, gw), 0) % sub
    for g in range(c // gw):
        cols = slice(g * gw, (g + 1) * gw)
        r = jax.nn.sigmoid(jnp.dot(xcb[:, cols], wa_ref[g], preferred_element_type=F32) + ba_ref[:, cols])
        ig = jax.nn.sigmoid(jnp.dot(xcb[:, cols], wx_ref[g], preferred_element_type=F32) + bx_ref[:, cols])
        log_a = (-RG_C * r) * jax.nn.softplus(-lam_ref[:, cols])
        a = jnp.exp(log_a)
        u = jnp.sqrt(-jnp.tanh(log_a) * (a * a + 1.0)) * (ig * xc[:, cols])
        for s in (1, 2, 4):
            keep = row_in_tile >= s
            u = jnp.where(keep, a * pltpu.roll(u, s, 0) + u, u)
            a = jnp.where(keep, a * pltpu.roll(a, s, 0), a)
        a_ref[:, cols] = a
        b_ref[:, cols] = u

    def tile_body(t, h):
        rows = pl.ds(pl.multiple_of(t * sub, sub), sub)
        hr = a_ref[rows, :] * h + b_ref[rows, :]
        b_ref[rows, :] = hr
        return jnp.broadcast_to(hr[sub - 1:sub, :], hr.shape)

    h_ref[...] = lax.fori_loop(0, tm // sub, tile_body, h_ref[...], unroll=4)

    gate = ag_ref[...]
    cdf = 0.5 * (1.0 + jnp.tanh(math.sqrt(2.0 / math.pi) * (gate + 0.044715 * (gate * gate * gate))))
    o_ref[...] = (gate * cdf * b_ref[...]).astype(o_ref.dtype)


def _rglru(ga, conv_w, conv_b, wa, wx, ba, bx, lam):
    m = ga.shape[0]
    c = ga.shape[1] // 2
    tm = min(256, m)
    assert m % tm == 0
    gw = wa.shape[1]
    const2 = lambda i: (0, 0)
    vmem = 2 * (2 * _nbytes((tm, c), F32) + _nbytes((tm, c), BF16) + 2 * _nbytes(wa.shape, BF16)) \
        + 3 * _nbytes((tm + 8, c), F32) + 12 * _nbytes((tm, c), F32)
    return pl.pallas_call(
        _rglru_kernel,
        grid=(m // tm,),
        in_specs=[pl.BlockSpec((tm, c), lambda i: (i, 0)),
                  pl.BlockSpec((tm, c), lambda i: (i, 1)),
                  pl.BlockSpec((CONV_W, c), const2), pl.BlockSpec((1, c), const2),
                  pl.BlockSpec(wa.shape, lambda i: (0, 0, 0)), pl.BlockSpec(wx.shape, lambda i: (0, 0, 0)),
                  pl.BlockSpec((1, c), const2), pl.BlockSpec((1, c), const2), pl.BlockSpec((1, c), const2)],
        out_specs=pl.BlockSpec((tm, c), lambda i: (i, 0)),
        out_shape=jax.ShapeDtypeStruct((m, c), BF16),
        scratch_shapes=[pltpu.VMEM((tm + V7X_SUBLANES, c), F32), pltpu.VMEM((tm, c), F32),
                        pltpu.VMEM((tm, c), F32), pltpu.VMEM((V7X_SUBLANES, c), F32)],
        compiler_params=_params(("arbitrary",), vmem),
        name="rglru",
    )(ga, ga, conv_w, conv_b, wa, wx, ba, bx, lam)


def _block_diag_groups(w, group):
    n, bw, _ = w.shape
    per = group // bw
    eye = jnp.eye(per, dtype=w.dtype)
    wg = w.reshape(n // per, per, bw, bw)
    return jnp.einsum('gpab,pq->gpaqb', wg, eye).reshape(n // per, group, group)


def _dil_kernel(*refs, has_prev, regroup):
    q_ref, kc_ref, kp_ref, vc_ref, vp_ref = refs[:5]
    refs = refs[5:]
    if has_prev:
        op_ref, lp_ref = refs[:2]
        refs = refs[2:]
    if regroup:
        o_ref, l_ref, oslab_ref, lslab_ref = refs
    else:
        (o_ref,) = refs

    i = pl.program_id(1)
    tq = q_ref.shape[0]
    n_heads = q_ref.shape[1] // HEAD_DIM
    qi = lax.broadcasted_iota(jnp.int32, (BLK, 2 * BLK), 0)
    kj = lax.broadcasted_iota(jnp.int32, (BLK, 2 * BLK), 1)
    band = (kj >= qi) & (kj <= qi + BLK)
    contract_last = (((1,), (1,)), ((), ()))
    for b in range(tq // BLK):
        rows = slice(b * BLK, (b + 1) * BLK)
        valid = band & (kj >= jnp.where(i > 0, 0, BLK)) if b == 0 else band
        bias = jnp.where(valid, 0.0, NEG)
        for h in range(n_heads):
            cols = slice(h * HEAD_DIM, (h + 1) * HEAD_DIM)
            if b == 0:
                kk = jnp.concatenate([kp_ref[:, cols], kc_ref[0:BLK, cols]], axis=0)
                vv = jnp.concatenate([vp_ref[:, cols], vc_ref[0:BLK, cols]], axis=0)
            else:
                kk = kc_ref[(b - 1) * BLK:(b + 1) * BLK, cols]
                vv = vc_ref[(b - 1) * BLK:(b + 1) * BLK, cols]
            s = lax.dot_general(q_ref[rows, cols], kk, contract_last, preferred_element_type=F32) + bias
            m = jnp.max(s, axis=-1, keepdims=True)
            p = jnp.exp2(s - m)
            l = jnp.sum(p, axis=-1, keepdims=True)
            o = jnp.dot(p.astype(BF16), vv, preferred_element_type=F32) / l
            lse = jnp.broadcast_to(m + jnp.log2(l), (BLK, HEAD_DIM))
            if has_prev:
                lp = lp_ref[rows, cols]
                mx = jnp.maximum(lp, lse)
                wa = jnp.exp2(lp - mx)
                wb = jnp.exp2(lse - mx)
                tot = wa + wb
                o = (op_ref[rows, cols] * wa + o * wb) / tot
                lse = mx + jnp.log2(tot)
            if regroup:
                base = (b * n_heads + h) * BLK
                oslab_ref[base:base + BLK, :] = o
                lslab_ref[base:base + BLK, :] = lse
                part = BLK // regroup
                for c in range(regroup):
                    src = pl.ds(base + c, part, stride=regroup)
                    o_ref[c, b * part:(b + 1) * part, cols] = oslab_ref[src, :]
                    l_ref[c, b * part:(b + 1) * part, cols] = lslab_ref[src, :]
            else:
                o_ref[rows, cols] = o.astype(o_ref.dtype)


def _dilated_attention(qkv_by_dil):
    dils = [dil for _, dil in DIL_PATTERNS]
    s_len = qkv_by_dil[dils[0]].shape[1] * dils[0]
    c = qkv_by_dil[dils[0]].shape[2] // 3
    o = lse = None
    for idx, (window, dil) in enumerate(DIL_PATTERNS):
        assert window // dil == BLK
        has_prev, is_last = idx > 0, idx == len(DIL_PATTERNS) - 1
        regroup = 0 if is_last else dils[idx + 1] // dil
        length = s_len // dil
        tq = min(512, length)
        assert length % tq == 0 and tq % BLK == 0
        per = tq // BLK
        cur = lambda off: pl.BlockSpec((None, tq, c), lambda r, i, off=off: (r, i, off))
        prev = lambda off: pl.BlockSpec((None, BLK, c),
                                        lambda r, i, off=off: (r, jnp.maximum(i * per - 1, 0), off))
        in_specs = [cur(0), cur(1), prev(1), cur(2), prev(2)]
        args = [qkv_by_dil[dil]] * 5
        if has_prev:
            in_specs += [cur(0), cur(0)]
            args += [o, lse]
        scratch = []
        if is_last:
            out_specs = [pl.BlockSpec((None, tq, c), lambda r, i: (r, i, 0))]
            out_shape = [jax.ShapeDtypeStruct((dil, length, c), BF16)]
        else:
            assert dils[idx + 1] == regroup * dil and BLK % (regroup * V7X_SUBLANES) == 0
            grouped = pl.BlockSpec((regroup, None, tq // regroup, c), lambda r, i: (0, r, i, 0))
            out_specs = [grouped, grouped]
            out_shape = [jax.ShapeDtypeStruct((regroup, dil, length // regroup, c), F32)] * 2
            scratch = [pltpu.VMEM((per * (c // HEAD_DIM) * BLK, HEAD_DIM), F32)] * 2
        vmem = 2 * (3 * _nbytes((tq, c), BF16) + 2 * _nbytes((BLK, c), BF16) + 4 * _nbytes((tq, c), F32)) \
            + 2 * _nbytes((tq, c), F32) + 16 * _nbytes((BLK, 2 * BLK), F32) * (c // HEAD_DIM)
        res = pl.pallas_call(
            functools.partial(_dil_kernel, has_prev=has_prev, regroup=regroup),
            grid=(dil, length // tq),
            in_specs=in_specs,
            out_specs=out_specs,
            out_shape=out_shape,
            scratch_shapes=scratch,
            compiler_params=_params(("parallel", "parallel"), vmem),
            name=f"dilated_attn_d{dil}",
        )(*args)
        if is_last:
            return res[0].transpose(1, 0, 2).reshape(s_len, c)
        o, lse = (r.reshape(regroup * dil, length // regroup, c) for r in res)


def _diff_attn_kernel(q_ref, k_ref, vt_ref, lq1_ref, lk1_ref, lq2_ref, lk2_ref, g_ref, o_ref,
                      m_ref, l_ref, acc_ref, s_ref, smax_ref, *, lambda_init, tk):
    tq = q_ref.shape[0]
    assert tk % tq == 0
    q0 = pl.program_id(1) * tq
    jd = q0 // tk
    m_ref[...] = jnp.full_like(m_ref, NEG)
    l_ref[...] = jnp.zeros_like(l_ref)
    acc_ref[...] = jnp.zeros_like(acc_ref)
    contract_last = (((1,), (1,)), ((), ()))

    def scores(j, slot):
        kb = k_ref[pl.ds(pl.multiple_of(j * tk, tk), tk), :]
        for c in range(2):
            cols = slice(c * HEAD_DIM, (c + 1) * HEAD_DIM)
            s = lax.dot_general(kb[:, cols], q_ref[:, cols], contract_last, preferred_element_type=F32)
            s_ref[slot, c] = s
            smax_ref[slot, c] = jnp.max(s, axis=0, keepdims=True)

    def update(j, slot, masked):
        vt = vt_ref[j]
        for c in range(2):
            s = s_ref[slot, c]
            if masked:
                kj = lax.broadcasted_iota(jnp.int32, (tk, tq), 0)
                qi = lax.broadcasted_iota(jnp.int32, (tk, tq), 1)
                s = jnp.where(kj <= qi + (q0 - jd * tk), s, NEG)
                m_cur = jnp.max(s, axis=0, keepdims=True)
            else:
                m_cur = smax_ref[slot, c]
            m_prev = m_ref[c]
            m_new = jnp.maximum(m_prev, m_cur)
            alpha = jnp.exp2(m_prev - m_new)
            p = jnp.exp2(s - m_new)
            l_ref[c] = alpha * l_ref[c] + jnp.sum(p, axis=0, keepdims=True)
            m_ref[c] = m_new
            pv = jnp.dot(vt, p.astype(BF16), preferred_element_type=F32)
            acc_ref[c] = acc_ref[c] * alpha + pv

    scores(0, 0)

    def pair(p, carry):
        scores(2 * p + 1, 1)
        update(2 * p, 0, False)
        scores(2 * p + 2, 0)
        update(2 * p + 1, 1, False)
        return carry

    lax.fori_loop(0, jd // 2, pair, 0)

    @pl.when(jd % 2 == 0)
    def _():
        update(jd, 0, True)

    @pl.when(jd % 2 == 1)
    def _():
        scores(jd, 1)
        update(jd - 1, 0, False)
        update(jd, 1, True)

    lam = (jnp.exp(jnp.sum(lq1_ref[...] * lk1_ref[...], axis=-1, keepdims=True))
           - jnp.exp(jnp.sum(lq2_ref[...] * lk2_ref[...], axis=-1, keepdims=True)) + lambda_init)
    o = acc_ref[0] * (1.0 / l_ref[0]) - lam * (acc_ref[1] * (1.0 / l_ref[1]))
    g = jnp.concatenate([g_ref[...]] * (tq // HEAD_DIM), axis=1)
    o = o * lax.rsqrt(jnp.mean(o * o, axis=0, keepdims=True) + RMS_EPS) * g
    o_ref[...] = (o * (1.0 - lambda_init)).T.astype(o_ref.dtype)


def _diff_attention(qkv, lq1, lk1, lq2, lk2, g, lambda_init):
    s_len = qkv.shape[0]
    d = qkv.shape[1] // 3
    hw = 2 * HEAD_DIM
    n_heads = d // hw
    tq = tk = min(512, s_len)
    assert s_len % tk == 0 and tk % tq == 0
    n_kv = s_len // tk
    vt = qkv[:, 2 * d:].reshape(n_kv, tk, n_heads, hw).transpose(2, 0, 3, 1)
    g_cols = jnp.broadcast_to(g.reshape(hw, 1), (hw, HEAD_DIM))
    vec = pl.BlockSpec((1, HEAD_DIM), lambda h, i: (0, 0))
    kv_mode = pl.Buffered(1)
    vmem = 2 * _nbytes((s_len, hw), BF16) + 4 * _nbytes((tq, hw), BF16) \
        + 2 * (4 * _nbytes((V7X_SUBLANES, tq), F32) + _nbytes((tq, hw), F32)) + 16 * _nbytes((tq, tk), F32)
    return pl.pallas_call(
        functools.partial(_diff_attn_kernel, lambda_init=lambda_init, tk=tk),
        grid=(n_heads, s_len // tq),
        in_specs=[pl.BlockSpec((tq, hw), lambda h, i: (i, h)),
                  pl.BlockSpec((s_len, hw), lambda h, i: (0, n_heads + h), pipeline_mode=kv_mode),
                  pl.BlockSpec((None, n_kv, hw, tk), lambda h, i: (h, 0, 0, 0), pipeline_mode=kv_mode),
                  vec, vec, vec, vec,
                  pl.BlockSpec((hw, HEAD_DIM), lambda h, i: (0, 0))],
        out_specs=pl.BlockSpec((tq, hw), lambda h, i: (i, h)),
        out_shape=jax.ShapeDtypeStruct((s_len, d), BF16),
        scratch_shapes=[pltpu.VMEM((2, 1, tq), F32), pltpu.VMEM((2, 1, tq), F32),
                        pltpu.VMEM((2, hw, tq), F32), pltpu.VMEM((2, 2, tk, tq), F32),
                        pltpu.VMEM((2, 2, 1, tq), F32)],
        compiler_params=_params(("parallel", "parallel"), vmem),
        name="diff_attn",
    )(qkv, qkv, vt, lq1, lk1, lq2, lk2, g_cols)


def _rope_tables(positions, q_scale):
    inv = ROPE_THETA ** (-jnp.arange(0, ROT_DIM, 2, dtype=F32) / ROT_DIM)
    ang = positions.astype(F32)[:, None] * inv
    cos, sin = jnp.cos(ang), jnp.sin(ang)
    n = positions.shape[0]
    rest = HEAD_DIM - ROT_DIM
    c = jnp.concatenate([cos, cos, jnp.ones((n, rest), F32)], axis=1)
    s1 = jnp.concatenate([jnp.zeros((n, ROT_HALF), F32), sin, jnp.zeros((n, rest), F32)], axis=1)
    s2 = jnp.concatenate([-sin, jnp.zeros((n, ROT_HALF + rest), F32)], axis=1)
    zero = jnp.zeros_like(c)
    return (jnp.stack([c * q_scale, c, jnp.ones_like(c)]),
            jnp.stack([s1 * q_scale, s1, zero]),
            jnp.stack([s2 * q_scale, s2, zero]))


def kernel(x, positions, ev_w_in, ev_conv_w, ev_conv_b, ev_gate_a_w, ev_gate_a_b, ev_gate_x_w, ev_gate_x_b, ev_rg_lambda, ev_w_out, od_w_in, od_lambda_q1, od_lambda_k1, od_lambda_q2, od_lambda_k2, od_subln_g, od_w_out, ln_mix_g, ln_mix_b, ln_mlp_g, ln_mlp_b, mlp_w1, mlp_w2):
    batch, s_len, d = x.shape
    depth = ln_mix_g.shape[0]
    d_rnn = ev_conv_w.shape[-1]
    d_dil = (ev_w_in.shape[-1] - 2 * d_rnn) // 3
    alpha = (2 * depth) ** 0.25
    q_scale = LOG2E * HEAD_DIM ** -0.5
    row = lambda v: v.reshape(1, -1)

    outs = []
    for bi in range(batch):
        xf = x[bi]
        xb = xf.astype(BF16)
        tabs = _rope_tables(positions[bi], q_scale)
        for layer in range(depth):
            p = layer // 2
            if layer % 2 == 0:
                w_in = ev_w_in[p].astype(BF16)
                ga = _proj(xb, w_in[:, :2 * d_rnn], F32)
                dils = [dil for _, dil in DIL_PATTERNS]
                qkvs = _proj_rope(xb, w_in[:, 2 * d_rnn:], tabs, d_dil, 2 * d_dil, [dil for dil in dils if dil > 1])
                qkv_by_dil = dict(zip(dils, [qkvs[0][None]] + list(qkvs[1:])))
                wa = _block_diag_groups(ev_gate_a_w[p], V7X_MXU_DIM).astype(BF16)
                wx = _block_diag_groups(ev_gate_x_w[p], V7X_MXU_DIM).astype(BF16)
                ya = _rglru(ga, ev_conv_w[p], row(ev_conv_b[p]), wa, wx, row(ev_gate_a_b[p]),
                            row(ev_gate_x_b[p]), row(ev_rg_lambda[p]))
                yb = _dilated_attention(qkv_by_dil)
                w_out = ev_w_out[p].astype(BF16)
                ys, ws = [ya, yb], [w_out[:d_rnn], w_out[d_rnn:]]
            else:
                lambda_init = 0.8 - 0.6 * math.exp(-0.3 * layer)
                (qkv,) = _proj_rope(xb, od_w_in[p].astype(BF16), tabs, d, 2 * d)
                y = _diff_attention(qkv, row(od_lambda_q1[p]), row(od_lambda_k1[p]), row(od_lambda_q2[p]),
                                    row(od_lambda_k2[p]), row(od_subln_g[p]), lambda_init)
                ys, ws = [y], [od_w_out[p].astype(BF16)]
            xf, xb = _outproj_ln(ys, ws, xf, row(ln_mix_g[layer]), row(ln_mix_b[layer]), alpha)
            xf, xb = _mlp_ln(xb, mlp_w1[layer].astype(BF16), mlp_w2[layer].astype(BF16), xf,
                             row(ln_mlp_g[layer]), row(ln_mlp_b[layer]), alpha)
        outs.append(xf)
    return outs[0][None] if batch == 1 else jnp.stack(outs, axis=0)
```

```python
import functools
import math

import jax
import jax.numpy as jnp
from jax import lax
from jax.experimental import pallas as pl
from jax.experimental.pallas import tpu as pltpu

HEAD_DIM = 128
ROT_DIM = HEAD_DIM // 4
ROT_HALF = ROT_DIM // 2
ROPE_THETA = 500000.0
CONV_W = 4
RG_C = 8.0
RNN_BLOCK_W = 64
DIL_PATTERNS = ((128, 1), (512, 4), (2048, 16))
BLK = 128
LN_EPS = 1e-5
RMS_EPS = 1e-5
NEG = -1e30
LOG2E = 1.4426950408889634

V7X_VMEM_BYTES = 64 * 2**20
V7X_VMEM_RESERVED_BYTES = 8 * 2**20
V7X_SUBLANES = 8
V7X_MXU_DIM = 256

F32 = jnp.float32
BF16 = jnp.bfloat16


def _params(semantics, vmem_bytes):
    limit = min(int(vmem_bytes), V7X_VMEM_BYTES - V7X_VMEM_RESERVED_BYTES)
    return pltpu.CompilerParams(dimension_semantics=semantics, vmem_limit_bytes=limit)


def _nbytes(shape, dtype):
    return math.prod(shape) * jnp.dtype(dtype).itemsize


def _ln_rows(z, g, b):
    mu = jnp.mean(z, axis=-1, keepdims=True)
    zc = z - mu
    var = jnp.mean(zc * zc, axis=-1, keepdims=True)
    return zc * lax.rsqrt(var + LN_EPS) * g + b


def _proj_kernel(x_ref, w_ref, o_ref):
    o_ref[...] = jnp.dot(x_ref[...], w_ref[...], preferred_element_type=F32).astype(o_ref.dtype)


def _proj_rope_kernel(x_ref, w_ref, c_ref, s1_ref, s2_ref, o_ref, *rest, dilations):
    dil_refs, slab_ref = rest[:len(dilations)], (rest[-1] if dilations else None)
    tm = x_ref.shape[0]
    acc = jnp.dot(x_ref[...], w_ref[...], preferred_element_type=F32)
    c, s1, s2 = c_ref[...], s1_ref[...], s2_ref[...]
    for h in range(acc.shape[1] // HEAD_DIM):
        cols = slice(h * HEAD_DIM, (h + 1) * HEAD_DIM)
        a = acc[:, cols]
        r = a * c + pltpu.roll(a, ROT_HALF, 1) * s1 + pltpu.roll(a, HEAD_DIM - ROT_HALF, 1) * s2
        o_ref[:, cols] = r.astype(o_ref.dtype)
        if dilations:
            slab_ref[h * tm:(h + 1) * tm, :] = r
            for d, d_ref in zip(dilations, dil_refs):
                for rho in range(d):
                    rows = pl.ds(h * tm + rho, tm // d, stride=d)
                    d_ref[rho, :, cols] = slab_ref[rows, :].astype(d_ref.dtype)


def _proj_tiles(m, n):
    tm = min(1024, m)
    tn = min(1024, n)
    assert m % tm == 0 and n % tn == 0
    return tm, tn


def _proj(xb, w, layer, col0, n, out_dtype):
    m, k = xb.shape
    tm, tn = _proj_tiles(m, n)
    assert col0 % tn == 0
    vmem = 2 * (_nbytes((tm, k), BF16) + _nbytes((k, tn), BF16) + _nbytes((tm, tn), out_dtype)) \
        + 2 * _nbytes((tm, tn), F32)
    return pl.pallas_call(
        _proj_kernel,
        grid=(m // tm, n // tn),
        in_specs=[pl.BlockSpec((tm, k), lambda i, j: (i, 0)),
                  pl.BlockSpec((None, k, tn), lambda i, j: (layer, 0, j + col0 // tn))],
        out_specs=pl.BlockSpec((tm, tn), lambda i, j: (i, j)),
        out_shape=jax.ShapeDtypeStruct((m, n), out_dtype),
        compiler_params=_params(("parallel", "arbitrary"), vmem),
        name="proj",
    )(xb, w)


def _proj_rope(xb, w, layer, col0, n, rope_tabs, n_q_cols, n_rope_cols, dilations=()):
    m, k = xb.shape
    tm, tn = _proj_tiles(m, n)
    assert n_q_cols % tn == 0 and n_rope_cols % tn == 0 and col0 % tn == 0
    n_q, n_rope = n_q_cols // tn, n_rope_cols // tn
    variant = lambda j: jnp.where(j < n_q, 0, jnp.where(j < n_rope, 1, 2))
    tab_spec = pl.BlockSpec((None, tm, HEAD_DIM), lambda i, j: (variant(j), i, 0))
    out_specs = [pl.BlockSpec((tm, tn), lambda i, j: (i, j))]
    out_shape = [jax.ShapeDtypeStruct((m, n), BF16)]
    for d in dilations:
        assert tm % (d * 2 * V7X_SUBLANES) == 0
        out_specs.append(pl.BlockSpec((d, tm // d, tn), lambda i, j: (0, i, j)))
        out_shape.append(jax.ShapeDtypeStruct((d, m // d, n), BF16))
    scratch = [pltpu.VMEM((tn // HEAD_DIM * tm, HEAD_DIM), F32)] if dilations else []
    vmem = 2 * (_nbytes((tm, k), BF16) + _nbytes((k, tn), BF16) + (1 + len(dilations)) * _nbytes((tm, tn), BF16)
                + 3 * _nbytes((tm, HEAD_DIM), F32)) + (3 + bool(dilations)) * _nbytes((tm, tn), F32)
    return pl.pallas_call(
        functools.partial(_proj_rope_kernel, dilations=tuple(dilations)),
        grid=(m // tm, n // tn),
        in_specs=[pl.BlockSpec((tm, k), lambda i, j: (i, 0)),
                  pl.BlockSpec((None, k, tn), lambda i, j: (layer, 0, j + col0 // tn)),
                  tab_spec, tab_spec, tab_spec],
        out_specs=out_specs,
        out_shape=out_shape,
        scratch_shapes=scratch,
        compiler_params=_params(("parallel", "arbitrary"), vmem),
        name="proj_rope",
    )(xb, w, *rope_tabs)


def _outproj_ln_kernel(*refs, n_in, alpha, sub):
    ys, ws = refs[:n_in], refs[n_in:2 * n_in]
    x_ref, g_ref, b_ref, of_ref, ob_ref = refs[2 * n_in:]
    for r in range(x_ref.shape[0] // sub):
        rows = slice(r * sub, (r + 1) * sub)
        acc = jnp.dot(ys[0][rows, :], ws[0][...], preferred_element_type=F32)
        for y_ref, w_ref in zip(ys[1:], ws[1:]):
            acc = acc + jnp.dot(y_ref[rows, :], w_ref[...], preferred_element_type=F32)
        y = _ln_rows(alpha * x_ref[rows, :] + acc, g_ref[...], b_ref[...])
        of_ref[rows, :] = y
        ob_ref[rows, :] = y.astype(BF16)


def _outproj_ln(ys, w, layer, xf, g, b, alpha):
    m, d = xf.shape
    tm = min(512, m)
    assert m % tm == 0
    row = lambda i: (i, 0)
    const = lambda i: (0, 0)
    in_specs = [pl.BlockSpec((tm, y.shape[1]), row) for y in ys]
    row0 = 0
    for y in ys:
        rows = y.shape[1]
        assert row0 % rows == 0
        in_specs.append(pl.BlockSpec((None, rows, d), lambda i, blk=row0 // rows: (layer, blk, 0)))
        row0 += rows
    assert row0 == w.shape[1]
    in_specs += [pl.BlockSpec((tm, d), row), pl.BlockSpec((1, d), const), pl.BlockSpec((1, d), const)]
    vmem = 2 * (sum(_nbytes((tm, y.shape[1]), BF16) for y in ys) + _nbytes(w.shape[1:], BF16)
                + 2 * _nbytes((tm, d), F32) + _nbytes((tm, d), BF16)) + 3 * _nbytes((tm, d), F32)
    return pl.pallas_call(
        functools.partial(_outproj_ln_kernel, n_in=len(ys), alpha=alpha, sub=min(V7X_MXU_DIM, tm)),
        grid=(m // tm,),
        in_specs=in_specs,
        out_specs=[pl.BlockSpec((tm, d), row), pl.BlockSpec((tm, d), row)],
        out_shape=[jax.ShapeDtypeStruct((m, d), F32), jax.ShapeDtypeStruct((m, d), BF16)],
        compiler_params=_params(("parallel",), vmem),
        name="outproj_ln",
    )(*ys, *([w] * len(ys)), xf, g, b)


def _mlp_ln_kernel(xb_ref, w1_ref, w2_ref, x_ref, g_ref, b_ref, of_ref, ob_ref, acc_ref, *, alpha, sub):
    j = pl.program_id(1)
    last = pl.num_programs(1) - 1

    def hidden():
        h = jnp.maximum(jnp.dot(xb_ref[...], w1_ref[...], preferred_element_type=F32), 0.0)
        return (h * h).astype(BF16)

    @pl.when(j == 0)
    def _():
        acc_ref[...] = jnp.zeros_like(acc_ref)

    @pl.when(j < last)
    def _():
        acc_ref[...] += jnp.dot(hidden(), w2_ref[...], preferred_element_type=F32)

    @pl.when(j == last)
    def _():
        h = hidden()
        for r in range(x_ref.shape[0] // sub):
            rows = slice(r * sub, (r + 1) * sub)
            z = acc_ref[rows, :] + jnp.dot(h[rows, :], w2_ref[...], preferred_element_type=F32)
            y = _ln_rows(alpha * x_ref[rows, :] + z, g_ref[...], b_ref[...])
            of_ref[rows, :] = y
            ob_ref[rows, :] = y.astype(BF16)


def _mlp_ln(xb, w1, w2, layer, xf, g, b, alpha):
    m, d = xf.shape
    f = w1.shape[2]
    tm = min(512, m)
    tf = min(1024, f)
    assert m % tm == 0 and f % tf == 0
    row = lambda i, j: (i, 0)
    const = lambda i, j: (0, 0)
    vmem = 2 * (_nbytes((tm, d), BF16) + 2 * _nbytes((d, tf), BF16) + 2 * _nbytes((tm, d), F32)
                + _nbytes((tm, d), BF16)) + 2 * _nbytes((tm, d), F32) + 4 * _nbytes((tm, tf), F32)
    return pl.pallas_call(
        functools.partial(_mlp_ln_kernel, alpha=alpha, sub=min(V7X_MXU_DIM, tm)),
        grid=(m // tm, f // tf),
        in_specs=[pl.BlockSpec((tm, d), row),
                  pl.BlockSpec((None, d, tf), lambda i, j: (layer, 0, j)),
                  pl.BlockSpec((None, tf, d), lambda i, j: (layer, j, 0)),
                  pl.BlockSpec((tm, d), row),
                  pl.BlockSpec((1, d), const), pl.BlockSpec((1, d), const)],
        out_specs=[pl.BlockSpec((tm, d), row), pl.BlockSpec((tm, d), row)],
        out_shape=[jax.ShapeDtypeStruct((m, d), F32), jax.ShapeDtypeStruct((m, d), BF16)],
        scratch_shapes=[pltpu.VMEM((tm, d), F32)],
        compiler_params=_params(("parallel", "arbitrary"), vmem),
        name="mlp_ln",
    )(xb, w1, w2, xf, g, b)


def _rglru_kernel(ag_ref, ax_ref, cw_ref, cb_ref, wa_ref, wx_ref, ba_ref, bx_ref, lam_ref,
                  o_ref, xs_ref, a_ref, b_ref, h_ref):
    i = pl.program_id(0)
    tm, c = ax_ref.shape
    sub = V7X_SUBLANES
    gw = wa_ref.shape[1]

    @pl.when(i == 0)
    def _():
        xs_ref[0:sub, :] = jnp.zeros((sub, c), F32)
        h_ref[...] = jnp.zeros_like(h_ref)

    x = ax_ref[...]
    xs_ref[sub:sub + tm, :] = x
    xc = cb_ref[...] + cw_ref[CONV_W - 1:CONV_W, :] * x
    for k in range(1, CONV_W):
        xc = xc + cw_ref[CONV_W - 1 - k:CONV_W - k, :] * xs_ref[pl.ds(sub - k, tm), :]
    xs_ref[0:sub, :] = x[tm - sub:tm, :]

    xcb = xc.astype(BF16)
    row_in_tile = lax.broadcasted_iota(jnp.int32, (tm, gw), 0) % sub
    for g in range(c // gw):
        cols = slice(g * gw, (g + 1) * gw)
        r = jax.nn.sigmoid(jnp.dot(xcb[:, cols], wa_ref[g], preferred_element_type=F32) + ba_ref[:, cols])
        ig = jax.nn.sigmoid(jnp.dot(xcb[:, cols], wx_ref[g], preferred_element_type=F32) + bx_ref[:, cols])
        log_a = (-RG_C * r) * jax.nn.softplus(-lam_ref[:, cols])
        a = jnp.exp(log_a)
        u = jnp.sqrt(-jnp.tanh(log_a) * (a * a + 1.0)) * (ig * xc[:, cols])
        for s in (1, 2, 4):
            keep = row_in_tile >= s
            u = jnp.where(keep, a * pltpu.roll(u, s, 0) + u, u)
            a = jnp.where(keep, a * pltpu.roll(a, s, 0), a)
        a_ref[:, cols] = a
        b_ref[:, cols] = u

    def tile_body(t, h):
        rows = pl.ds(pl.multiple_of(t * sub, sub), sub)
        hr = a_ref[rows, :] * h + b_ref[rows, :]
        b_ref[rows, :] = hr
        return jnp.broadcast_to(hr[sub - 1:sub, :], hr.shape)

    h_ref[...] = lax.fori_loop(0, tm // sub, tile_body, h_ref[...], unroll=4)

    gate = ag_ref[...]
    cdf = 0.5 * (1.0 + jnp.tanh(math.sqrt(2.0 / math.pi) * (gate + 0.044715 * (gate * gate * gate))))
    o_ref[...] = (gate * cdf * b_ref[...]).astype(o_ref.dtype)


def _rglru(ga, conv_w, conv_b, wa, wx, ba, bx, lam):
    m = ga.shape[0]
    c = ga.shape[1] // 2
    tm = min(256, m)
    assert m % tm == 0
    gw = wa.shape[1]
    const2 = lambda i: (0, 0)
    vmem = 2 * (2 * _nbytes((tm, c), F32) + _nbytes((tm, c), BF16) + 2 * _nbytes(wa.shape, BF16)) \
        + 3 * _nbytes((tm + 8, c), F32) + 12 * _nbytes((tm, c), F32)
    return pl.pallas_call(
        _rglru_kernel,
        grid=(m // tm,),
        in_specs=[pl.BlockSpec((tm, c), lambda i: (i, 0)),
                  pl.BlockSpec((tm, c), lambda i: (i, 1)),
                  pl.BlockSpec((CONV_W, c), const2), pl.BlockSpec((1, c), const2),
                  pl.BlockSpec(wa.shape, lambda i: (0, 0, 0)), pl.BlockSpec(wx.shape, lambda i: (0, 0, 0)),
                  pl.BlockSpec((1, c), const2), pl.BlockSpec((1, c), const2), pl.BlockSpec((1, c), const2)],
        out_specs=pl.BlockSpec((tm, c), lambda i: (i, 0)),
        out_shape=jax.ShapeDtypeStruct((m, c), BF16),
        scratch_shapes=[pltpu.VMEM((tm + V7X_SUBLANES, c), F32), pltpu.VMEM((tm, c), F32),
                        pltpu.VMEM((tm, c), F32), pltpu.VMEM((V7X_SUBLANES, c), F32)],
        compiler_params=_params(("arbitrary",), vmem),
        name="rglru",
    )(ga, ga, conv_w, conv_b, wa, wx, ba, bx, lam)


def _block_diag_groups(w, group):
    n, bw, _ = w.shape
    per = group // bw
    eye = jnp.eye(per, dtype=w.dtype)
    wg = w.reshape(n // per, per, bw, bw)
    return jnp.einsum('gpab,pq->gpaqb', wg, eye).reshape(n // per, group, group)


def _dil_kernel(*refs, has_prev, regroup):
    q_ref, kc_ref, kp_ref, vc_ref, vp_ref = refs[:5]
    refs = refs[5:]
    if has_prev:
        op_ref, lp_ref = refs[:2]
        refs = refs[2:]
    if regroup:
        o_ref, l_ref, oslab_ref, lslab_ref = refs
    else:
        (o_ref,) = refs

    i = pl.program_id(1)
    tq = q_ref.shape[0]
    n_heads = q_ref.shape[1] // HEAD_DIM
    qi = lax.broadcasted_iota(jnp.int32, (BLK, 2 * BLK), 0)
    kj = lax.broadcasted_iota(jnp.int32, (BLK, 2 * BLK), 1)
    band = (kj >= qi) & (kj <= qi + BLK)
    contract_last = (((1,), (1,)), ((), ()))
    for b in range(tq // BLK):
        rows = slice(b * BLK, (b + 1) * BLK)
        valid = band & (kj >= jnp.where(i > 0, 0, BLK)) if b == 0 else band
        bias = jnp.where(valid, 0.0, NEG)
        for h in range(n_heads):
            cols = slice(h * HEAD_DIM, (h + 1) * HEAD_DIM)
            if b == 0:
                kk = jnp.concatenate([kp_ref[:, cols], kc_ref[0:BLK, cols]], axis=0)
                vv = jnp.concatenate([vp_ref[:, cols], vc_ref[0:BLK, cols]], axis=0)
            else:
                kk = kc_ref[(b - 1) * BLK:(b + 1) * BLK, cols]
                vv = vc_ref[(b - 1) * BLK:(b + 1) * BLK, cols]
            s = lax.dot_general(q_ref[rows, cols], kk, contract_last, preferred_element_type=F32) + bias
            m = jnp.max(s, axis=-1, keepdims=True)
            p = jnp.exp2(s - m)
            l = jnp.sum(p, axis=-1, keepdims=True)
            o = jnp.dot(p.astype(BF16), vv, preferred_element_type=F32) / l
            lse = jnp.broadcast_to(m + jnp.log2(l), (BLK, HEAD_DIM))
            if has_prev:
                lp = lp_ref[rows, cols]
                mx = jnp.maximum(lp, lse)
                wa = jnp.exp2(lp - mx)
                wb = jnp.exp2(lse - mx)
                tot = wa + wb
                o = (op_ref[rows, cols] * wa + o * wb) / tot
                lse = mx + jnp.log2(tot)
            if regroup:
                base = (b * n_heads + h) * BLK
                oslab_ref[base:base + BLK, :] = o
                lslab_ref[base:base + BLK, :] = lse
                part = BLK // regroup
                for c in range(regroup):
                    src = pl.ds(base + c, part, stride=regroup)
                    o_ref[c, b * part:(b + 1) * part, cols] = oslab_ref[src, :]
                    l_ref[c, b * part:(b + 1) * part, cols] = lslab_ref[src, :]
            else:
                o_ref[rows, cols] = o.astype(o_ref.dtype)


def _dilated_attention(qkv_by_dil):
    dils = [dil for _, dil in DIL_PATTERNS]
    s_len = qkv_by_dil[dils[0]].shape[1] * dils[0]
    c = qkv_by_dil[dils[0]].shape[2] // 3
    o = lse = None
    for idx, (window, dil) in enumerate(DIL_PATTERNS):
        assert window // dil == BLK
        has_prev, is_last = idx > 0, idx == len(DIL_PATTERNS) - 1
        regroup = 0 if is_last else dils[idx + 1] // dil
        length = s_len // dil
        tq = min(512, length)
        assert length % tq == 0 and tq % BLK == 0
        per = tq // BLK
        cur = lambda off: pl.BlockSpec((None, tq, c), lambda r, i, off=off: (r, i, off))
        prev = lambda off: pl.BlockSpec((None, BLK, c),
                                        lambda r, i, off=off: (r, jnp.maximum(i * per - 1, 0), off))
        in_specs = [cur(0), cur(1), prev(1), cur(2), prev(2)]
        args = [qkv_by_dil[dil]] * 5
        if has_prev:
            in_specs += [cur(0), cur(0)]
            args += [o, lse]
        scratch = []
        if is_last:
            out_specs = [pl.BlockSpec((None, tq, c), lambda r, i: (r, i, 0))]
            out_shape = [jax.ShapeDtypeStruct((dil, length, c), BF16)]
        else:
            assert dils[idx + 1] == regroup * dil and BLK % (regroup * V7X_SUBLANES) == 0
            grouped = pl.BlockSpec((regroup, None, tq // regroup, c), lambda r, i: (0, r, i, 0))
            out_specs = [grouped, grouped]
            out_shape = [jax.ShapeDtypeStruct((regroup, dil, length // regroup, c), F32)] * 2
            scratch = [pltpu.VMEM((per * (c // HEAD_DIM) * BLK, HEAD_DIM), F32)] * 2
        vmem = 2 * (3 * _nbytes((tq, c), BF16) + 2 * _nbytes((BLK, c), BF16) + 4 * _nbytes((tq, c), F32)) \
            + 2 * _nbytes((tq, c), F32) + 16 * _nbytes((BLK, 2 * BLK), F32) * (c // HEAD_DIM)
        res = pl.pallas_call(
            functools.partial(_dil_kernel, has_prev=has_prev, regroup=regroup),
            grid=(dil, length // tq),
            in_specs=in_specs,
            out_specs=out_specs,
            out_shape=out_shape,
            scratch_shapes=scratch,
            compiler_params=_params(("parallel", "parallel"), vmem),
            name=f"dilated_attn_d{dil}",
        )(*args)
        if is_last:
            return res[0].transpose(1, 0, 2).reshape(s_len, c)
        o, lse = (r.reshape(regroup * dil, length // regroup, c) for r in res)


def _diff_attn_kernel(q_ref, k_ref, vt_ref, lq1_ref, lk1_ref, lq2_ref, lk2_ref, g_ref, o_ref,
                      m_ref, l_ref, acc_ref, s_ref, smax_ref, *, lambda_init, tk):
    tq = q_ref.shape[0]
    assert tk % tq == 0
    q0 = pl.program_id(1) * tq
    jd = q0 // tk
    m_ref[...] = jnp.full_like(m_ref, NEG)
    l_ref[...] = jnp.zeros_like(l_ref)
    acc_ref[...] = jnp.zeros_like(acc_ref)
    contract_last = (((1,), (1,)), ((), ()))

    def scores(j, slot):
        kb = k_ref[pl.ds(pl.multiple_of(j * tk, tk), tk), :]
        for c in range(2):
            cols = slice(c * HEAD_DIM, (c + 1) * HEAD_DIM)
            s = lax.dot_general(kb[:, cols], q_ref[:, cols], contract_last, preferred_element_type=F32)
            s_ref[slot, c] = s
            smax_ref[slot, c] = jnp.max(s, axis=0, keepdims=True)

    def update(j, slot, masked):
        vt = vt_ref[j]
        for c in range(2):
            s = s_ref[slot, c]
            if masked:
                kj = lax.broadcasted_iota(jnp.int32, (tk, tq), 0)
                qi = lax.broadcasted_iota(jnp.int32, (tk, tq), 1)
                s = jnp.where(kj <= qi + (q0 - jd * tk), s, NEG)
                m_cur = jnp.max(s, axis=0, keepdims=True)
            else:
                m_cur = smax_ref[slot, c]
            m_prev = m_ref[c]
            m_new = jnp.maximum(m_prev, m_cur)
            alpha = jnp.exp2(m_prev - m_new)
            p = jnp.exp2(s - m_new)
            l_ref[c] = alpha * l_ref[c] + jnp.sum(p, axis=0, keepdims=True)
            m_ref[c] = m_new
            pv = jnp.dot(vt, p.astype(BF16), preferred_element_type=F32)
            acc_ref[c] = acc_ref[c] * alpha + pv

    scores(0, 0)

    def pair(p, carry):
        scores(2 * p + 1, 1)
        update(2 * p, 0, False)
        scores(2 * p + 2, 0)
        update(2 * p + 1, 1, False)
        return carry

    lax.fori_loop(0, jd // 2, pair, 0)

    @pl.when(jd % 2 == 0)
    def _():
        update(jd, 0, True)

    @pl.when(jd % 2 == 1)
    def _():
        scores(jd, 1)
        update(jd - 1, 0, False)
        update(jd, 1, True)

    lam = (jnp.exp(jnp.sum(lq1_ref[...] * lk1_ref[...], axis=-1, keepdims=True))
           - jnp.exp(jnp.sum(lq2_ref[...] * lk2_ref[...], axis=-1, keepdims=True)) + lambda_init)
    o = acc_ref[0] * (1.0 / l_ref[0]) - lam * (acc_ref[1] * (1.0 / l_ref[1]))
    g = jnp.concatenate([g_ref[...]] * (tq // HEAD_DIM), axis=1)
    o = o * lax.rsqrt(jnp.mean(o * o, axis=0, keepdims=True) + RMS_EPS) * g
    o_ref[...] = (o * (1.0 - lambda_init)).T.astype(o_ref.dtype)


def _diff_attention(qkv, lq1, lk1, lq2, lk2, g, lambda_init):
    s_len = qkv.shape[0]
    d = qkv.shape[1] // 3
    hw = 2 * HEAD_DIM
    n_heads = d // hw
    tq = tk = min(1024, s_len)
    assert s_len % tk == 0 and tk % tq == 0
    n_kv = s_len // tk
    vt = qkv[:, 2 * d:].reshape(n_kv, tk, n_heads, hw).transpose(2, 0, 3, 1)
    g_cols = jnp.broadcast_to(g.reshape(hw, 1), (hw, HEAD_DIM))
    vec = pl.BlockSpec((1, HEAD_DIM), lambda h, i: (0, 0))
    kv_mode = pl.Buffered(1)
    vmem = 2 * _nbytes((s_len, hw), BF16) + 4 * _nbytes((tq, hw), BF16) \
        + 2 * (4 * _nbytes((V7X_SUBLANES, tq), F32) + _nbytes((tq, hw), F32)) + 16 * _nbytes((tq, tk), F32)
    return pl.pallas_call(
        functools.partial(_diff_attn_kernel, lambda_init=lambda_init, tk=tk),
        grid=(n_heads, s_len // tq),
        in_specs=[pl.BlockSpec((tq, hw), lambda h, i: (i, h)),
                  pl.BlockSpec((s_len, hw), lambda h, i: (0, n_heads + h), pipeline_mode=kv_mode),
                  pl.BlockSpec((None, n_kv, hw, tk), lambda h, i: (h, 0, 0, 0), pipeline_mode=kv_mode),
                  vec, vec, vec, vec,
                  pl.BlockSpec((hw, HEAD_DIM), lambda h, i: (0, 0))],
        out_specs=pl.BlockSpec((tq, hw), lambda h, i: (i, h)),
        out_shape=jax.ShapeDtypeStruct((s_len, d), BF16),
        scratch_shapes=[pltpu.VMEM((2, 1, tq), F32), pltpu.VMEM((2, 1, tq), F32),
                        pltpu.VMEM((2, hw, tq), F32), pltpu.VMEM((2, 2, tk, tq), F32),
                        pltpu.VMEM((2, 2, 1, tq), F32)],
        compiler_params=_params(("parallel", "parallel"), vmem),
        name="diff_attn",
    )(qkv, qkv, vt, lq1, lk1, lq2, lk2, g_cols)


def _rope_tables(positions, q_scale):
    inv = ROPE_THETA ** (-jnp.arange(0, ROT_DIM, 2, dtype=F32) / ROT_DIM)
    ang = positions.astype(F32)[:, None] * inv
    cos, sin = jnp.cos(ang), jnp.sin(ang)
    n = positions.shape[0]
    rest = HEAD_DIM - ROT_DIM
    c = jnp.concatenate([cos, cos, jnp.ones((n, rest), F32)], axis=1)
    s1 = jnp.concatenate([jnp.zeros((n, ROT_HALF), F32), sin, jnp.zeros((n, rest), F32)], axis=1)
    s2 = jnp.concatenate([-sin, jnp.zeros((n, ROT_HALF + rest), F32)], axis=1)
    zero = jnp.zeros_like(c)
    return (jnp.stack([c * q_scale, c, jnp.ones_like(c)]),
            jnp.stack([s1 * q_scale, s1, zero]),
            jnp.stack([s2 * q_scale, s2, zero]))


def kernel(x, positions, ev_w_in, ev_conv_w, ev_conv_b, ev_gate_a_w, ev_gate_a_b, ev_gate_x_w, ev_gate_x_b, ev_rg_lambda, ev_w_out, od_w_in, od_lambda_q1, od_lambda_k1, od_lambda_q2, od_lambda_k2, od_subln_g, od_w_out, ln_mix_g, ln_mix_b, ln_mlp_g, ln_mlp_b, mlp_w1, mlp_w2):
    batch, s_len, d = x.shape
    depth = ln_mix_g.shape[0]
    d_rnn = ev_conv_w.shape[-1]
    d_dil = (ev_w_in.shape[-1] - 2 * d_rnn) // 3
    alpha = (2 * depth) ** 0.25
    q_scale = LOG2E * HEAD_DIM ** -0.5
    row = lambda v: v.reshape(1, -1)

    ev_w_in_b, ev_w_out_b = ev_w_in.astype(BF16), ev_w_out.astype(BF16)
    od_w_in_b, od_w_out_b = od_w_in.astype(BF16), od_w_out.astype(BF16)
    mlp_w1_b, mlp_w2_b = mlp_w1.astype(BF16), mlp_w2.astype(BF16)
    dils = [dil for _, dil in DIL_PATTERNS]

    outs = []
    for bi in range(batch):
        xf = x[bi]
        xb = xf.astype(BF16)
        tabs = _rope_tables(positions[bi], q_scale)
        for layer in range(depth):
            p = layer // 2
            if layer % 2 == 0:
                ga = _proj(xb, ev_w_in_b, p, 0, 2 * d_rnn, F32)
                qkvs = _proj_rope(xb, ev_w_in_b, p, 2 * d_rnn, 3 * d_dil, tabs, d_dil, 2 * d_dil,
                                  [dil for dil in dils if dil > 1])
                qkv_by_dil = dict(zip(dils, [qkvs[0][None]] + list(qkvs[1:])))
                wa = _block_diag_groups(ev_gate_a_w[p], V7X_MXU_DIM).astype(BF16)
                wx = _block_diag_groups(ev_gate_x_w[p], V7X_MXU_DIM).astype(BF16)
                ya = _rglru(ga, ev_conv_w[p], row(ev_conv_b[p]), wa, wx, row(ev_gate_a_b[p]),
                            row(ev_gate_x_b[p]), row(ev_rg_lambda[p]))
                yb = _dilated_attention(qkv_by_dil)
                ys, w_out = [ya, yb], ev_w_out_b
            else:
                lambda_init = 0.8 - 0.6 * math.exp(-0.3 * layer)
                (qkv,) = _proj_rope(xb, od_w_in_b, p, 0, 3 * d, tabs, d, 2 * d)
                y = _diff_attention(qkv, row(od_lambda_q1[p]), row(od_lambda_k1[p]), row(od_lambda_q2[p]),
                                    row(od_lambda_k2[p]), row(od_subln_g[p]), lambda_init)
                ys, w_out = [y], od_w_out_b
            xf, xb = _outproj_ln(ys, w_out, p, xf, row(ln_mix_g[layer]), row(ln_mix_b[layer]), alpha)
            xf, xb = _mlp_ln(xb, mlp_w1_b, mlp_w2_b, layer, xf, row(ln_mlp_g[layer]), row(ln_mlp_b[layer]), alpha)
        outs.append(xf)
    return outs[0][None] if batch == 1 else jnp.stack(outs, axis=0)
```

```python
import functools
import math

import jax
import jax.numpy as jnp
from jax import lax
from jax.experimental import pallas as pl
from jax.experimental.pallas import tpu as pltpu

HEAD_DIM = 128
ROT_DIM = HEAD_DIM // 4
ROT_HALF = ROT_DIM // 2
ROPE_THETA = 500000.0
CONV_W = 4
RG_C = 8.0
RNN_BLOCK_W = 64
DIL_PATTERNS = ((128, 1), (512, 4), (2048, 16))
BLK = 128
LN_EPS = 1e-5
RMS_EPS = 1e-5
NEG = -1e30
LOG2E = 1.4426950408889634

V7X_VMEM_BYTES = 64 * 2**20
V7X_VMEM_RESERVED_BYTES = 8 * 2**20
V7X_SUBLANES = 8
V7X_MXU_DIM = 256

PROJ_ROWS = 1024
PROJ_COLS = 1024
PROJ_COLS_WIDE = 2048
OUTPROJ_ROWS = 512
MLP_ROWS = 512
MLP_HIDDEN_COLS = 1024
LN_SUB_ROWS = V7X_MXU_DIM
RGLRU_ROWS = 256
DIL_ROWS = 512
ATTN_TILE = 1024

F32 = jnp.float32
BF16 = jnp.bfloat16


def _params(semantics, vmem_bytes):
    limit = min(int(vmem_bytes), V7X_VMEM_BYTES - V7X_VMEM_RESERVED_BYTES)
    return pltpu.CompilerParams(dimension_semantics=semantics, vmem_limit_bytes=limit)


def _nbytes(shape, dtype):
    return math.prod(shape) * jnp.dtype(dtype).itemsize


def _ln_rows(z, g, b):
    mu = jnp.mean(z, axis=-1, keepdims=True)
    zc = z - mu
    var = jnp.mean(zc * zc, axis=-1, keepdims=True)
    return zc * lax.rsqrt(var + LN_EPS) * g + b


def _proj_kernel(x_ref, w_ref, o_ref):
    o_ref[...] = jnp.dot(x_ref[...], w_ref[...], preferred_element_type=F32).astype(o_ref.dtype)


def _proj_rope_kernel(x_ref, w_ref, c_ref, s1_ref, s2_ref, o_ref, *rest, dilations):
    dil_refs, slab_refs = rest[:len(dilations)], rest[len(dilations):]
    tm = x_ref.shape[0]
    acc = jnp.dot(x_ref[...], w_ref[...], preferred_element_type=F32)
    c, s1, s2 = c_ref[...], s1_ref[...], s2_ref[...]
    for h in range(acc.shape[1] // HEAD_DIM):
        cols = slice(h * HEAD_DIM, (h + 1) * HEAD_DIM)
        a = acc[:, cols]
        r = a * c + pltpu.roll(a, ROT_HALF, 1) * s1 + pltpu.roll(a, HEAD_DIM - ROT_HALF, 1) * s2
        o_ref[:, cols] = r.astype(o_ref.dtype)
        if dilations:
            base, src_ref, src_d = h * tm, slab_refs[0], 1
            src_ref[base:base + tm, :] = r
            for idx, (d, d_ref) in enumerate(zip(dilations, dil_refs)):
                f, n_rows = d // src_d, tm // d
                dst_ref = slab_refs[(idx + 1) % 2]
                for rho_src in range(src_d):
                    for k in range(f):
                        rho = k * src_d + rho_src
                        val = src_ref[pl.ds(base + rho_src * (tm // src_d) + k, n_rows, stride=f), :]
                        d_ref[rho, :, cols] = val.astype(d_ref.dtype)
                        if idx + 1 < len(dilations):
                            dst_ref[base + rho * n_rows:base + (rho + 1) * n_rows, :] = val
                src_ref, src_d = dst_ref, d


def _proj_tiles(m, n, wide):
    tm = min(PROJ_ROWS, m)
    tn = min(PROJ_COLS_WIDE if wide else PROJ_COLS, n)
    assert m % tm == 0 and n % tn == 0
    return tm, tn


def _proj(xb, w, layer, col0, n, out_dtype):
    m, k = xb.shape
    tm, tn = _proj_tiles(m, n, wide=True)
    assert col0 % tn == 0
    vmem = 2 * (_nbytes((tm, k), BF16) + _nbytes((k, tn), BF16) + _nbytes((tm, tn), out_dtype)) \
        + 2 * _nbytes((tm, tn), F32)
    return pl.pallas_call(
        _proj_kernel,
        grid=(m // tm, n // tn),
        in_specs=[pl.BlockSpec((tm, k), lambda i, j: (i, 0)),
                  pl.BlockSpec((None, k, tn), lambda i, j: (layer, 0, j + col0 // tn))],
        out_specs=pl.BlockSpec((tm, tn), lambda i, j: (i, j)),
        out_shape=jax.ShapeDtypeStruct((m, n), out_dtype),
        compiler_params=_params(("parallel", "arbitrary"), vmem),
        name="proj",
    )(xb, w)


def _proj_rope(xb, w, layer, col0, n, rope_tabs, n_q_cols, n_rope_cols, dilations=()):
    m, k = xb.shape
    tm, tn = _proj_tiles(m, n, wide=not dilations)
    assert n_q_cols % tn == 0 and n_rope_cols % tn == 0 and col0 % tn == 0
    n_q, n_rope = n_q_cols // tn, n_rope_cols // tn
    variant = lambda j: jnp.where(j < n_q, 0, jnp.where(j < n_rope, 1, 2))
    tab_spec = pl.BlockSpec((None, tm, HEAD_DIM), lambda i, j: (variant(j), i, 0))
    out_specs = [pl.BlockSpec((tm, tn), lambda i, j: (i, j))]
    out_shape = [jax.ShapeDtypeStruct((m, n), BF16)]
    for prev_d, d in zip((1,) + tuple(dilations), dilations):
        assert d % prev_d == 0 and tm % (d * 2 * V7X_SUBLANES) == 0
        out_specs.append(pl.BlockSpec((d, tm // d, tn), lambda i, j: (0, i, j)))
        out_shape.append(jax.ShapeDtypeStruct((d, m // d, n), BF16))
    scratch = [pltpu.VMEM((tn // HEAD_DIM * tm, HEAD_DIM), F32)] * (2 if dilations else 0)
    vmem = 2 * (_nbytes((tm, k), BF16) + _nbytes((k, tn), BF16) + (1 + len(dilations)) * _nbytes((tm, tn), BF16)
                + 3 * _nbytes((tm, HEAD_DIM), F32)) + (3 + 2 * bool(dilations)) * _nbytes((tm, tn), F32)
    return pl.pallas_call(
        functools.partial(_proj_rope_kernel, dilations=tuple(dilations)),
        grid=(m // tm, n // tn),
        in_specs=[pl.BlockSpec((tm, k), lambda i, j: (i, 0)),
                  pl.BlockSpec((None, k, tn), lambda i, j: (layer, 0, j + col0 // tn)),
                  tab_spec, tab_spec, tab_spec],
        out_specs=out_specs,
        out_shape=out_shape,
        scratch_shapes=scratch,
        compiler_params=_params(("parallel", "arbitrary"), vmem),
        name="proj_rope",
    )(xb, w, *rope_tabs)


def _outproj_ln_kernel(*refs, n_in, alpha, sub):
    ys, ws = refs[:n_in], refs[n_in:2 * n_in]
    x_ref, g_ref, b_ref, of_ref, ob_ref = refs[2 * n_in:]
    for r in range(x_ref.shape[0] // sub):
        rows = slice(r * sub, (r + 1) * sub)
        acc = jnp.dot(ys[0][rows, :], ws[0][...], preferred_element_type=F32)
        for y_ref, w_ref in zip(ys[1:], ws[1:]):
            acc = acc + jnp.dot(y_ref[rows, :], w_ref[...], preferred_element_type=F32)
        y = _ln_rows(alpha * x_ref[rows, :] + acc, g_ref[...], b_ref[...])
        of_ref[rows, :] = y
        ob_ref[rows, :] = y.astype(BF16)


def _outproj_ln(ys, w, layer, xf, g, b, alpha):
    m, d = xf.shape
    tm = min(OUTPROJ_ROWS, m)
    assert m % tm == 0
    row = lambda i: (i, 0)
    const = lambda i: (0, 0)
    in_specs = [pl.BlockSpec((tm, y.shape[1]), row) for y in ys]
    row0 = 0
    for y in ys:
        rows = y.shape[1]
        assert row0 % rows == 0
        in_specs.append(pl.BlockSpec((None, rows, d), lambda i, blk=row0 // rows: (layer, blk, 0)))
        row0 += rows
    assert row0 == w.shape[1]
    in_specs += [pl.BlockSpec((tm, d), row), pl.BlockSpec((1, d), const), pl.BlockSpec((1, d), const)]
    vmem = 2 * (sum(_nbytes((tm, y.shape[1]), BF16) for y in ys) + _nbytes(w.shape[1:], BF16)
                + 2 * _nbytes((tm, d), F32) + _nbytes((tm, d), BF16)) + 3 * _nbytes((tm, d), F32)
    return pl.pallas_call(
        functools.partial(_outproj_ln_kernel, n_in=len(ys), alpha=alpha, sub=min(LN_SUB_ROWS, tm)),
        grid=(m // tm,),
        in_specs=in_specs,
        out_specs=[pl.BlockSpec((tm, d), row), pl.BlockSpec((tm, d), row)],
        out_shape=[jax.ShapeDtypeStruct((m, d), F32), jax.ShapeDtypeStruct((m, d), BF16)],
        compiler_params=_params(("parallel",), vmem),
        name="outproj_ln",
    )(*ys, *([w] * len(ys)), xf, g, b)


def _mlp_ln_kernel(xb_ref, w1_ref, w2_ref, x_ref, g_ref, b_ref, of_ref, ob_ref, acc_ref, *, alpha, sub):
    j = pl.program_id(1)
    last = pl.num_programs(1) - 1

    def hidden():
        h = jnp.maximum(jnp.dot(xb_ref[...], w1_ref[...], preferred_element_type=F32), 0.0)
        return (h * h).astype(BF16)

    @pl.when(j == 0)
    def _():
        acc_ref[...] = jnp.zeros_like(acc_ref)

    @pl.when(j < last)
    def _():
        acc_ref[...] += jnp.dot(hidden(), w2_ref[...], preferred_element_type=F32)

    @pl.when(j == last)
    def _():
        h = hidden()
        for r in range(x_ref.shape[0] // sub):
            rows = slice(r * sub, (r + 1) * sub)
            z = acc_ref[rows, :] + jnp.dot(h[rows, :], w2_ref[...], preferred_element_type=F32)
            y = _ln_rows(alpha * x_ref[rows, :] + z, g_ref[...], b_ref[...])
            of_ref[rows, :] = y
            ob_ref[rows, :] = y.astype(BF16)


def _mlp_ln(xb, w1, w2, layer, xf, g, b, alpha):
    m, d = xf.shape
    f = w1.shape[2]
    tm = min(MLP_ROWS, m)
    tf = min(MLP_HIDDEN_COLS, f)
    assert m % tm == 0 and f % tf == 0
    row = lambda i, j: (i, 0)
    const = lambda i, j: (0, 0)
    vmem = 2 * (_nbytes((tm, d), BF16) + 2 * _nbytes((d, tf), BF16) + 2 * _nbytes((tm, d), F32)
                + _nbytes((tm, d), BF16)) + 2 * _nbytes((tm, d), F32) + 4 * _nbytes((tm, tf), F32)
    return pl.pallas_call(
        functools.partial(_mlp_ln_kernel, alpha=alpha, sub=min(LN_SUB_ROWS, tm)),
        grid=(m // tm, f // tf),
        in_specs=[pl.BlockSpec((tm, d), row),
                  pl.BlockSpec((None, d, tf), lambda i, j: (layer, 0, j)),
                  pl.BlockSpec((None, tf, d), lambda i, j: (layer, j, 0)),
                  pl.BlockSpec((tm, d), row),
                  pl.BlockSpec((1, d), const), pl.BlockSpec((1, d), const)],
        out_specs=[pl.BlockSpec((tm, d), row), pl.BlockSpec((tm, d), row)],
        out_shape=[jax.ShapeDtypeStruct((m, d), F32), jax.ShapeDtypeStruct((m, d), BF16)],
        scratch_shapes=[pltpu.VMEM((tm, d), F32)],
        compiler_params=_params(("parallel", "arbitrary"), vmem),
        name="mlp_ln",
    )(xb, w1, w2, xf, g, b)


def _rglru_kernel(ag_ref, ax_ref, cw_ref, cb_ref, wa_ref, wx_ref, ba_ref, bx_ref, lam_ref,
                  o_ref, xs_ref, a_ref, b_ref, h_ref):
    i = pl.program_id(0)
    tm, c = ax_ref.shape
    sub = V7X_SUBLANES
    gw = wa_ref.shape[1]

    @pl.when(i == 0)
    def _():
        xs_ref[0:sub, :] = jnp.zeros((sub, c), F32)
        h_ref[...] = jnp.zeros_like(h_ref)

    x = ax_ref[...]
    xs_ref[sub:sub + tm, :] = x
    xc = cb_ref[...] + cw_ref[CONV_W - 1:CONV_W, :] * x
    for k in range(1, CONV_W):
        xc = xc + cw_ref[CONV_W - 1 - k:CONV_W - k, :] * xs_ref[pl.ds(sub - k, tm), :]
    xs_ref[0:sub, :] = x[tm - sub:tm, :]

    xcb = xc.astype(BF16)
    row_in_tile = lax.broadcasted_iota(jnp.int32, (tm, gw), 0) % sub
    for g in range(c // gw):
        cols = slice(g * gw, (g + 1) * gw)
        r = jax.nn.sigmoid(jnp.dot(xcb[:, cols], wa_ref[g], preferred_element_type=F32) + ba_ref[:, cols])
        ig = jax.nn.sigmoid(jnp.dot(xcb[:, cols], wx_ref[g], preferred_element_type=F32) + bx_ref[:, cols])
        log_a = (-RG_C * r) * jax.nn.softplus(-lam_ref[:, cols])
        a = jnp.exp(log_a)
        u = jnp.sqrt(-jnp.tanh(log_a) * (a * a + 1.0)) * (ig * xc[:, cols])
        for s in (1, 2, 4):
            keep = row_in_tile >= s
            u = jnp.where(keep, a * pltpu.roll(u, s, 0) + u, u)
            a = jnp.where(keep, a * pltpu.roll(a, s, 0), a)
        a_ref[:, cols] = a
        b_ref[:, cols] = u

    def tile_body(t, h):
        rows = pl.ds(pl.multiple_of(t * sub, sub), sub)
        hr = a_ref[rows, :] * h + b_ref[rows, :]
        b_ref[rows, :] = hr
        return jnp.broadcast_to(hr[sub - 1:sub, :], hr.shape)

    h_ref[...] = lax.fori_loop(0, tm // sub, tile_body, h_ref[...], unroll=4)

    gate = ag_ref[...]
    cdf = 0.5 * (1.0 + jnp.tanh(math.sqrt(2.0 / math.pi) * (gate + 0.044715 * (gate * gate * gate))))
    o_ref[...] = (gate * cdf * b_ref[...]).astype(o_ref.dtype)


def _rglru(ga, conv_w, conv_b, wa, wx, ba, bx, lam):
    m = ga.shape[0]
    c = ga.shape[1] // 2
    tm = min(RGLRU_ROWS, m)
    assert m % tm == 0
    gw = wa.shape[1]
    const2 = lambda i: (0, 0)
    vmem = 2 * (2 * _nbytes((tm, c), F32) + _nbytes((tm, c), BF16) + 2 * _nbytes(wa.shape, BF16)) \
        + 3 * _nbytes((tm + V7X_SUBLANES, c), F32) + 12 * _nbytes((tm, c), F32)
    return pl.pallas_call(
        _rglru_kernel,
        grid=(m // tm,),
        in_specs=[pl.BlockSpec((tm, c), lambda i: (i, 0)),
                  pl.BlockSpec((tm, c), lambda i: (i, 1)),
                  pl.BlockSpec((CONV_W, c), const2), pl.BlockSpec((1, c), const2),
                  pl.BlockSpec(wa.shape, lambda i: (0, 0, 0)), pl.BlockSpec(wx.shape, lambda i: (0, 0, 0)),
                  pl.BlockSpec((1, c), const2), pl.BlockSpec((1, c), const2), pl.BlockSpec((1, c), const2)],
        out_specs=pl.BlockSpec((tm, c), lambda i: (i, 0)),
        out_shape=jax.ShapeDtypeStruct((m, c), BF16),
        scratch_shapes=[pltpu.VMEM((tm + V7X_SUBLANES, c), F32), pltpu.VMEM((tm, c), F32),
                        pltpu.VMEM((tm, c), F32), pltpu.VMEM((V7X_SUBLANES, c), F32)],
        compiler_params=_params(("arbitrary",), vmem),
        name="rglru",
    )(ga, ga, conv_w, conv_b, wa, wx, ba, bx, lam)


def _block_diag_groups(w, group):
    n, bw, _ = w.shape
    per = group // bw
    eye = jnp.eye(per, dtype=w.dtype)
    wg = w.reshape(n // per, per, bw, bw)
    return jnp.einsum('gpab,pq->gpaqb', wg, eye).reshape(n // per, group, group)


def _dil_kernel(*refs, has_prev, regroup):
    q_ref, kc_ref, kp_ref, vc_ref, vp_ref = refs[:5]
    refs = refs[5:]
    if has_prev:
        op_ref, lp_ref = refs[:2]
        refs = refs[2:]
    if regroup:
        o_ref, l_ref, oslab_ref, lslab_ref = refs
    else:
        (o_ref,) = refs

    i = pl.program_id(1)
    tq = q_ref.shape[0]
    n_heads = q_ref.shape[1] // HEAD_DIM
    qi = lax.broadcasted_iota(jnp.int32, (BLK, 2 * BLK), 0)
    kj = lax.broadcasted_iota(jnp.int32, (BLK, 2 * BLK), 1)
    band = (kj >= qi) & (kj <= qi + BLK)
    contract_last = (((1,), (1,)), ((), ()))
    for b in range(tq // BLK):
        rows = slice(b * BLK, (b + 1) * BLK)
        valid = band & (kj >= jnp.where(i > 0, 0, BLK)) if b == 0 else band
        bias = jnp.where(valid, 0.0, NEG)
        for h in range(n_heads):
            cols = slice(h * HEAD_DIM, (h + 1) * HEAD_DIM)
            if b == 0:
                kk = jnp.concatenate([kp_ref[:, cols], kc_ref[0:BLK, cols]], axis=0)
                vv = jnp.concatenate([vp_ref[:, cols], vc_ref[0:BLK, cols]], axis=0)
            else:
                kk = kc_ref[(b - 1) * BLK:(b + 1) * BLK, cols]
                vv = vc_ref[(b - 1) * BLK:(b + 1) * BLK, cols]
            s = lax.dot_general(q_ref[rows, cols], kk, contract_last, preferred_element_type=F32) + bias
            m = jnp.max(s, axis=-1, keepdims=True)
            p = jnp.exp2(s - m)
            l = jnp.sum(p, axis=-1, keepdims=True)
            o = jnp.dot(p.astype(BF16), vv, preferred_element_type=F32) / l
            lse = jnp.broadcast_to(m + jnp.log2(l), (BLK, HEAD_DIM))
            if has_prev:
                lp = lp_ref[rows, cols]
                mx = jnp.maximum(lp, lse)
                wa = jnp.exp2(lp - mx)
                wb = jnp.exp2(lse - mx)
                tot = wa + wb
                o = (op_ref[rows, cols] * wa + o * wb) / tot
                lse = mx + jnp.log2(tot)
            if regroup:
                base = (b * n_heads + h) * BLK
                oslab_ref[base:base + BLK, :] = o
                lslab_ref[base:base + BLK, :] = lse
                part = BLK // regroup
                for c in range(regroup):
                    src = pl.ds(base + c, part, stride=regroup)
                    o_ref[c, b * part:(b + 1) * part, cols] = oslab_ref[src, :]
                    l_ref[c, b * part:(b + 1) * part, cols] = lslab_ref[src, :]
            else:
                o_ref[rows, cols] = o.astype(o_ref.dtype)


def _dilated_attention(qkv_by_dil):
    dils = [dil for _, dil in DIL_PATTERNS]
    s_len = qkv_by_dil[dils[0]].shape[1] * dils[0]
    c = qkv_by_dil[dils[0]].shape[2] // 3
    o = lse = None
    for idx, (window, dil) in enumerate(DIL_PATTERNS):
        assert window // dil == BLK
        has_prev, is_last = idx > 0, idx == len(DIL_PATTERNS) - 1
        regroup = 0 if is_last else dils[idx + 1] // dil
        length = s_len // dil
        tq = min(DIL_ROWS, length)
        assert length % tq == 0 and tq % BLK == 0
        per = tq // BLK
        cur = lambda off: pl.BlockSpec((None, tq, c), lambda r, i, off=off: (r, i, off))
        prev = lambda off: pl.BlockSpec((None, BLK, c),
                                        lambda r, i, off=off: (r, jnp.maximum(i * per - 1, 0), off))
        in_specs = [cur(0), cur(1), prev(1), cur(2), prev(2)]
        args = [qkv_by_dil[dil]] * 5
        if has_prev:
            in_specs += [cur(0), cur(0)]
            args += [o, lse]
        scratch = []
        if is_last:
            out_specs = [pl.BlockSpec((None, tq, c), lambda r, i: (r, i, 0))]
            out_shape = [jax.ShapeDtypeStruct((dil, length, c), BF16)]
        else:
            assert dils[idx + 1] == regroup * dil and BLK % (regroup * V7X_SUBLANES) == 0
            grouped = pl.BlockSpec((regroup, None, tq // regroup, c), lambda r, i: (0, r, i, 0))
            out_specs = [grouped, grouped]
            out_shape = [jax.ShapeDtypeStruct((regroup, dil, length // regroup, c), F32)] * 2
            scratch = [pltpu.VMEM((per * (c // HEAD_DIM) * BLK, HEAD_DIM), F32)] * 2
        vmem = 2 * (3 * _nbytes((tq, c), BF16) + 2 * _nbytes((BLK, c), BF16) + 4 * _nbytes((tq, c), F32)) \
            + 2 * _nbytes((tq, c), F32) + 16 * _nbytes((BLK, 2 * BLK), F32) * (c // HEAD_DIM)
        res = pl.pallas_call(
            functools.partial(_dil_kernel, has_prev=has_prev, regroup=regroup),
            grid=(dil, length // tq),
            in_specs=in_specs,
            out_specs=out_specs,
            out_shape=out_shape,
            scratch_shapes=scratch,
            compiler_params=_params(("parallel", "parallel"), vmem),
            name=f"dilated_attn_d{dil}",
        )(*args)
        if is_last:
            return res[0].transpose(1, 0, 2).reshape(s_len, c)
        o, lse = (r.reshape(regroup * dil, length // regroup, c) for r in res)


def _diff_attn_kernel(q_ref, k_ref, vt_ref, lq1_ref, lk1_ref, lq2_ref, lk2_ref, g_ref, o_ref,
                      m_ref, l_ref, acc_ref, s_ref, smax_ref, *, lambda_init, tk):
    tq = q_ref.shape[0]
    assert tk % tq == 0
    q0 = pl.program_id(1) * tq
    jd = q0 // tk
    m_ref[...] = jnp.full_like(m_ref, NEG)
    l_ref[...] = jnp.zeros_like(l_ref)
    acc_ref[...] = jnp.zeros_like(acc_ref)
    contract_last = (((1,), (1,)), ((), ()))

    def scores(j, slot):
        kb = k_ref[pl.ds(pl.multiple_of(j * tk, tk), tk), :]
        for c in range(2):
            cols = slice(c * HEAD_DIM, (c + 1) * HEAD_DIM)
            s = lax.dot_general(kb[:, cols], q_ref[:, cols], contract_last, preferred_element_type=F32)
            s_ref[slot, c] = s
            smax_ref[slot, c] = jnp.max(s, axis=0, keepdims=True)

    def update(j, slot, masked):
        vt = vt_ref[j]
        for c in range(2):
            s = s_ref[slot, c]
            if masked:
                kj = lax.broadcasted_iota(jnp.int32, (tk, tq), 0)
                qi = lax.broadcasted_iota(jnp.int32, (tk, tq), 1)
                s = jnp.where(kj <= qi + (q0 - jd * tk), s, NEG)
                m_cur = jnp.max(s, axis=0, keepdims=True)
            else:
                m_cur = smax_ref[slot, c]
            m_prev = m_ref[c]
            m_new = jnp.maximum(m_prev, m_cur)
            alpha = jnp.exp2(m_prev - m_new)
            p = jnp.exp2(s - m_new)
            l_ref[c] = alpha * l_ref[c] + jnp.sum(p, axis=0, keepdims=True)
            m_ref[c] = m_new
            pv = jnp.dot(vt, p.astype(BF16), preferred_element_type=F32)
            acc_ref[c] = acc_ref[c] * alpha + pv

    scores(0, 0)

    def pair(p, carry):
        scores(2 * p + 1, 1)
        update(2 * p, 0, False)
        scores(2 * p + 2, 0)
        update(2 * p + 1, 1, False)
        return carry

    lax.fori_loop(0, jd // 2, pair, 0)

    @pl.when(jd % 2 == 0)
    def _():
        update(jd, 0, True)

    @pl.when(jd % 2 == 1)
    def _():
        scores(jd, 1)
        update(jd - 1, 0, False)
        update(jd, 1, True)

    lam = (jnp.exp(jnp.sum(lq1_ref[...] * lk1_ref[...], axis=-1, keepdims=True))
           - jnp.exp(jnp.sum(lq2_ref[...] * lk2_ref[...], axis=-1, keepdims=True)) + lambda_init)
    o = acc_ref[0] * (1.0 / l_ref[0]) - lam * (acc_ref[1] * (1.0 / l_ref[1]))
    g = jnp.concatenate([g_ref[...]] * (tq // HEAD_DIM), axis=1)
    o = o * lax.rsqrt(jnp.mean(o * o, axis=0, keepdims=True) + RMS_EPS) * g
    o_ref[...] = (o * (1.0 - lambda_init)).T.astype(o_ref.dtype)


def _diff_attention(qkv, lq1, lk1, lq2, lk2, g, lambda_init):
    s_len = qkv.shape[0]
    d = qkv.shape[1] // 3
    hw = 2 * HEAD_DIM
    n_heads = d // hw
    tq = tk = min(ATTN_TILE, s_len)
    assert s_len % tk == 0 and tk % tq == 0
    n_kv = s_len // tk
    vt = qkv[:, 2 * d:].reshape(n_kv, tk, n_heads, hw).transpose(2, 0, 3, 1)
    g_cols = jnp.broadcast_to(g.reshape(hw, 1), (hw, HEAD_DIM))
    vec = pl.BlockSpec((1, HEAD_DIM), lambda h, i: (0, 0))
    kv_mode = pl.Buffered(1)
    vmem = 2 * _nbytes((s_len, hw), BF16) + 4 * _nbytes((tq, hw), BF16) \
        + 2 * (4 * _nbytes((V7X_SUBLANES, tq), F32) + _nbytes((tq, hw), F32)) + 16 * _nbytes((tq, tk), F32)
    return pl.pallas_call(
        functools.partial(_diff_attn_kernel, lambda_init=lambda_init, tk=tk),
        grid=(n_heads, s_len // tq),
        in_specs=[pl.BlockSpec((tq, hw), lambda h, i: (i, h)),
                  pl.BlockSpec((s_len, hw), lambda h, i: (0, n_heads + h), pipeline_mode=kv_mode),
                  pl.BlockSpec((None, n_kv, hw, tk), lambda h, i: (h, 0, 0, 0), pipeline_mode=kv_mode),
                  vec, vec, vec, vec,
                  pl.BlockSpec((hw, HEAD_DIM), lambda h, i: (0, 0))],
        out_specs=pl.BlockSpec((tq, hw), lambda h, i: (i, h)),
        out_shape=jax.ShapeDtypeStruct((s_len, d), BF16),
        scratch_shapes=[pltpu.VMEM((2, 1, tq), F32), pltpu.VMEM((2, 1, tq), F32),
                        pltpu.VMEM((2, hw, tq), F32), pltpu.VMEM((2, 2, tk, tq), F32),
                        pltpu.VMEM((2, 2, 1, tq), F32)],
        compiler_params=_params(("parallel", "parallel"), vmem),
        name="diff_attn",
    )(qkv, qkv, vt, lq1, lk1, lq2, lk2, g_cols)


def _rope_tables(positions, q_scale):
    inv = ROPE_THETA ** (-jnp.arange(0, ROT_DIM, 2, dtype=F32) / ROT_DIM)
    ang = positions.astype(F32)[:, None] * inv
    cos, sin = jnp.cos(ang), jnp.sin(ang)
    n = positions.shape[0]
    rest = HEAD_DIM - ROT_DIM
    c = jnp.concatenate([cos, cos, jnp.ones((n, rest), F32)], axis=1)
    s1 = jnp.concatenate([jnp.zeros((n, ROT_HALF), F32), sin, jnp.zeros((n, rest), F32)], axis=1)
    s2 = jnp.concatenate([-sin, jnp.zeros((n, ROT_HALF + rest), F32)], axis=1)
    zero = jnp.zeros_like(c)
    return (jnp.stack([c * q_scale, c, jnp.ones_like(c)]),
            jnp.stack([s1 * q_scale, s1, zero]),
            jnp.stack([s2 * q_scale, s2, zero]))


def kernel(x, positions, ev_w_in, ev_conv_w, ev_conv_b, ev_gate_a_w, ev_gate_a_b, ev_gate_x_w, ev_gate_x_b, ev_rg_lambda, ev_w_out, od_w_in, od_lambda_q1, od_lambda_k1, od_lambda_q2, od_lambda_k2, od_subln_g, od_w_out, ln_mix_g, ln_mix_b, ln_mlp_g, ln_mlp_b, mlp_w1, mlp_w2):
    batch, s_len, d = x.shape
    depth = ln_mix_g.shape[0]
    d_rnn = ev_conv_w.shape[-1]
    d_dil = (ev_w_in.shape[-1] - 2 * d_rnn) // 3
    alpha = (2 * depth) ** 0.25
    q_scale = LOG2E * HEAD_DIM ** -0.5
    row = lambda v: v.reshape(1, -1)

    ev_w_in_b, ev_w_out_b = ev_w_in.astype(BF16), ev_w_out.astype(BF16)
    od_w_in_b, od_w_out_b = od_w_in.astype(BF16), od_w_out.astype(BF16)
    mlp_w1_b, mlp_w2_b = mlp_w1.astype(BF16), mlp_w2.astype(BF16)
    dils = [dil for _, dil in DIL_PATTERNS]

    outs = []
    for bi in range(batch):
        xf = x[bi]
        xb = xf.astype(BF16)
        tabs = _rope_tables(positions[bi], q_scale)
        for layer in range(depth):
            p = layer // 2
            if layer % 2 == 0:
                ga = _proj(xb, ev_w_in_b, p, 0, 2 * d_rnn, F32)
                qkvs = _proj_rope(xb, ev_w_in_b, p, 2 * d_rnn, 3 * d_dil, tabs, d_dil, 2 * d_dil,
                                  [dil for dil in dils if dil > 1])
                qkv_by_dil = dict(zip(dils, [qkvs[0][None]] + list(qkvs[1:])))
                wa = _block_diag_groups(ev_gate_a_w[p], V7X_MXU_DIM).astype(BF16)
                wx = _block_diag_groups(ev_gate_x_w[p], V7X_MXU_DIM).astype(BF16)
                ya = _rglru(ga, ev_conv_w[p], row(ev_conv_b[p]), wa, wx, row(ev_gate_a_b[p]),
                            row(ev_gate_x_b[p]), row(ev_rg_lambda[p]))
                yb = _dilated_attention(qkv_by_dil)
                ys, w_out = [ya, yb], ev_w_out_b
            else:
                lambda_init = 0.8 - 0.6 * math.exp(-0.3 * layer)
                (qkv,) = _proj_rope(xb, od_w_in_b, p, 0, 3 * d, tabs, d, 2 * d)
                y = _diff_attention(qkv, row(od_lambda_q1[p]), row(od_lambda_k1[p]), row(od_lambda_q2[p]),
                                    row(od_lambda_k2[p]), row(od_subln_g[p]), lambda_init)
                ys, w_out = [y], od_w_out_b
            xf, xb = _outproj_ln(ys, w_out, p, xf, row(ln_mix_g[layer]), row(ln_mix_b[layer]), alpha)
            xf, xb = _mlp_ln(xb, mlp_w1_b, mlp_w2_b, layer, xf, row(ln_mlp_g[layer]), row(ln_mlp_b[layer]), alpha)
        outs.append(xf)
    return outs[0][None] if batch == 1 else jnp.stack(outs, axis=0)
```

```python
import functools
import math

import jax
import jax.numpy as jnp
from jax import lax
from jax.experimental import pallas as pl
from jax.experimental.pallas import tpu as pltpu

HEAD_DIM = 128
ROT_DIM = HEAD_DIM // 4
ROT_HALF = ROT_DIM // 2
ROPE_THETA = 500000.0
CONV_W = 4
RG_C = 8.0
RNN_BLOCK_W = 64
DIL_PATTERNS = ((128, 1), (512, 4), (2048, 16))
BLK = 128
LN_EPS = 1e-5
RMS_EPS = 1e-5
NEG = -1e30
LOG2E = 1.4426950408889634

V7X_VMEM_BYTES = 64 * 2**20
V7X_VMEM_RESERVED_BYTES = 8 * 2**20
V7X_SUBLANES = 8
V7X_MXU_DIM = 256

PROJ_ROWS = 1024
PROJ_COLS = 1024
PROJ_COLS_WIDE = 2048
OUTPROJ_ROWS = 512
MLP_ROWS = 512
MLP_HIDDEN_COLS = 1024
LN_SUB_ROWS = V7X_MXU_DIM
RGLRU_ROWS = 256
DIL_ROWS = 512
ATTN_TILE = 1024

F32 = jnp.float32
BF16 = jnp.bfloat16


def _params(semantics, vmem_bytes):
    limit = min(int(vmem_bytes), V7X_VMEM_BYTES - V7X_VMEM_RESERVED_BYTES)
    return pltpu.CompilerParams(dimension_semantics=semantics, vmem_limit_bytes=limit)


def _nbytes(shape, dtype):
    return math.prod(shape) * jnp.dtype(dtype).itemsize


def _ln_rows(z, g, b):
    mu = jnp.mean(z, axis=-1, keepdims=True)
    zc = z - mu
    var = jnp.mean(zc * zc, axis=-1, keepdims=True)
    return zc * lax.rsqrt(var + LN_EPS) * g + b


def _proj_kernel(x_ref, w_ref, o_ref):
    o_ref[...] = jnp.dot(x_ref[...], w_ref[...], preferred_element_type=F32).astype(o_ref.dtype)


def _proj_rope_kernel(x_ref, w_ref, c_ref, s1_ref, s2_ref, o_ref, *rest, dilations):
    dil_refs, slab_refs = rest[:len(dilations)], rest[len(dilations):]
    tm = x_ref.shape[0]
    acc = jnp.dot(x_ref[...], w_ref[...], preferred_element_type=F32)
    c, s1, s2 = c_ref[...], s1_ref[...], s2_ref[...]
    for h in range(acc.shape[1] // HEAD_DIM):
        cols = slice(h * HEAD_DIM, (h + 1) * HEAD_DIM)
        a = acc[:, cols]
        r = a * c + pltpu.roll(a, ROT_HALF, 1) * s1 + pltpu.roll(a, HEAD_DIM - ROT_HALF, 1) * s2
        o_ref[:, cols] = r.astype(o_ref.dtype)
        if dilations:
            base, src_ref, src_d = h * tm, slab_refs[0], 1
            src_ref[base:base + tm, :] = r
            for idx, (d, d_ref) in enumerate(zip(dilations, dil_refs)):
                f, n_rows = d // src_d, tm // d
                dst_ref = slab_refs[(idx + 1) % 2]
                for rho_src in range(src_d):
                    for k in range(f):
                        rho = k * src_d + rho_src
                        val = src_ref[pl.ds(base + rho_src * (tm // src_d) + k, n_rows, stride=f), :]
                        d_ref[rho, :, cols] = val.astype(d_ref.dtype)
                        if idx + 1 < len(dilations):
                            dst_ref[base + rho * n_rows:base + (rho + 1) * n_rows, :] = val
                src_ref, src_d = dst_ref, d


def _proj_tiles(m, n, wide):
    tm = min(PROJ_ROWS, m)
    tn = min(PROJ_COLS_WIDE if wide else PROJ_COLS, n)
    assert m % tm == 0 and n % tn == 0
    return tm, tn


def _proj(xb, w, layer, col0, n, out_dtype):
    m, k = xb.shape
    tm, tn = _proj_tiles(m, n, wide=True)
    assert col0 % tn == 0
    vmem = 2 * (_nbytes((tm, k), BF16) + _nbytes((k, tn), BF16) + _nbytes((tm, tn), out_dtype)) \
        + 2 * _nbytes((tm, tn), F32)
    return pl.pallas_call(
        _proj_kernel,
        grid=(m // tm, n // tn),
        in_specs=[pl.BlockSpec((tm, k), lambda i, j: (i, 0)),
                  pl.BlockSpec((None, k, tn), lambda i, j: (layer, 0, j + col0 // tn))],
        out_specs=pl.BlockSpec((tm, tn), lambda i, j: (i, j)),
        out_shape=jax.ShapeDtypeStruct((m, n), out_dtype),
        compiler_params=_params(("parallel", "arbitrary"), vmem),
        name="proj",
    )(xb, w)


def _proj_rope(xb, w, layer, col0, n, rope_tabs, n_q_cols, n_rope_cols, dilations=()):
    m, k = xb.shape
    tm, tn = _proj_tiles(m, n, wide=not dilations)
    assert n_q_cols % tn == 0 and n_rope_cols % tn == 0 and col0 % tn == 0
    n_q, n_rope = n_q_cols // tn, n_rope_cols // tn
    variant = lambda j: jnp.where(j < n_q, 0, jnp.where(j < n_rope, 1, 2))
    tab_spec = pl.BlockSpec((None, tm, HEAD_DIM), lambda i, j: (variant(j), i, 0))
    out_specs = [pl.BlockSpec((tm, tn), lambda i, j: (i, j))]
    out_shape = [jax.ShapeDtypeStruct((m, n), BF16)]
    for prev_d, d in zip((1,) + tuple(dilations), dilations):
        assert d % prev_d == 0 and tm % (d * 2 * V7X_SUBLANES) == 0
        out_specs.append(pl.BlockSpec((d, tm // d, tn), lambda i, j: (0, i, j)))
        out_shape.append(jax.ShapeDtypeStruct((d, m // d, n), BF16))
    scratch = [pltpu.VMEM((tn // HEAD_DIM * tm, HEAD_DIM), F32)] * (2 if dilations else 0)
    vmem = 2 * (_nbytes((tm, k), BF16) + _nbytes((k, tn), BF16) + (1 + len(dilations)) * _nbytes((tm, tn), BF16)
                + 3 * _nbytes((tm, HEAD_DIM), F32)) + (3 + 2 * bool(dilations)) * _nbytes((tm, tn), F32)
    return pl.pallas_call(
        functools.partial(_proj_rope_kernel, dilations=tuple(dilations)),
        grid=(m // tm, n // tn),
        in_specs=[pl.BlockSpec((tm, k), lambda i, j: (i, 0)),
                  pl.BlockSpec((None, k, tn), lambda i, j: (layer, 0, j + col0 // tn)),
                  tab_spec, tab_spec, tab_spec],
        out_specs=out_specs,
        out_shape=out_shape,
        scratch_shapes=scratch,
        compiler_params=_params(("parallel", "arbitrary"), vmem),
        name="proj_rope",
    )(xb, w, *rope_tabs)


def _outproj_ln_kernel(*refs, n_in, alpha, sub):
    ys, ws = refs[:n_in], refs[n_in:2 * n_in]
    x_ref, g_ref, b_ref, of_ref, ob_ref = refs[2 * n_in:]
    for r in range(x_ref.shape[0] // sub):
        rows = slice(r * sub, (r + 1) * sub)
        acc = jnp.dot(ys[0][rows, :], ws[0][...], preferred_element_type=F32)
        for y_ref, w_ref in zip(ys[1:], ws[1:]):
            acc = acc + jnp.dot(y_ref[rows, :], w_ref[...], preferred_element_type=F32)
        y = _ln_rows(alpha * x_ref[rows, :] + acc, g_ref[...], b_ref[...])
        of_ref[rows, :] = y
        ob_ref[rows, :] = y.astype(BF16)


def _outproj_ln(ys, w, layer, xf, g, b, alpha):
    m, d = xf.shape
    tm = min(OUTPROJ_ROWS, m)
    assert m % tm == 0
    row = lambda i: (i, 0)
    const = lambda i: (0, 0)
    in_specs = [pl.BlockSpec((tm, y.shape[1]), row) for y in ys]
    row0 = 0
    for y in ys:
        rows = y.shape[1]
        assert row0 % rows == 0
        in_specs.append(pl.BlockSpec((None, rows, d), lambda i, blk=row0 // rows: (layer, blk, 0)))
        row0 += rows
    assert row0 == w.shape[1]
    in_specs += [pl.BlockSpec((tm, d), row), pl.BlockSpec((1, d), const), pl.BlockSpec((1, d), const)]
    vmem = 2 * (sum(_nbytes((tm, y.shape[1]), BF16) for y in ys) + _nbytes(w.shape[1:], BF16)
                + 2 * _nbytes((tm, d), F32) + _nbytes((tm, d), BF16)) + 3 * _nbytes((tm, d), F32)
    return pl.pallas_call(
        functools.partial(_outproj_ln_kernel, n_in=len(ys), alpha=alpha, sub=min(LN_SUB_ROWS, tm)),
        grid=(m // tm,),
        in_specs=in_specs,
        out_specs=[pl.BlockSpec((tm, d), row), pl.BlockSpec((tm, d), row)],
        out_shape=[jax.ShapeDtypeStruct((m, d), F32), jax.ShapeDtypeStruct((m, d), BF16)],
        compiler_params=_params(("parallel",), vmem),
        name="outproj_ln",
    )(*ys, *([w] * len(ys)), xf, g, b)


def _mlp_ln_kernel(xb_ref, w1_ref, w2_ref, x_ref, g_ref, b_ref, of_ref, ob_ref, acc_ref, *, alpha, sub):
    j = pl.program_id(1)
    last = pl.num_programs(1) - 1

    def hidden():
        h = jnp.maximum(jnp.dot(xb_ref[...], w1_ref[...], preferred_element_type=F32), 0.0)
        return (h * h).astype(BF16)

    @pl.when(j == 0)
    def _():
        acc_ref[...] = jnp.zeros_like(acc_ref)

    @pl.when(j < last)
    def _():
        acc_ref[...] += jnp.dot(hidden(), w2_ref[...], preferred_element_type=F32)

    @pl.when(j == last)
    def _():
        h = hidden()
        for r in range(x_ref.shape[0] // sub):
            rows = slice(r * sub, (r + 1) * sub)
            z = acc_ref[rows, :] + jnp.dot(h[rows, :], w2_ref[...], preferred_element_type=F32)
            y = _ln_rows(alpha * x_ref[rows, :] + z, g_ref[...], b_ref[...])
            of_ref[rows, :] = y
            ob_ref[rows, :] = y.astype(BF16)


def _mlp_ln(xb, w1, w2, layer, xf, g, b, alpha):
    m, d = xf.shape
    f = w1.shape[2]
    tm = min(MLP_ROWS, m)
    tf = min(MLP_HIDDEN_COLS, f)
    assert m % tm == 0 and f % tf == 0
    row = lambda i, j: (i, 0)
    const = lambda i, j: (0, 0)
    vmem = 2 * (_nbytes((tm, d), BF16) + 2 * _nbytes((d, tf), BF16) + 2 * _nbytes((tm, d), F32)
                + _nbytes((tm, d), BF16)) + 2 * _nbytes((tm, d), F32) + 4 * _nbytes((tm, tf), F32)
    return pl.pallas_call(
        functools.partial(_mlp_ln_kernel, alpha=alpha, sub=min(LN_SUB_ROWS, tm)),
        grid=(m // tm, f // tf),
        in_specs=[pl.BlockSpec((tm, d), row),
                  pl.BlockSpec((None, d, tf), lambda i, j: (layer, 0, j)),
                  pl.BlockSpec((None, tf, d), lambda i, j: (layer, j, 0)),
                  pl.BlockSpec((tm, d), row),
                  pl.BlockSpec((1, d), const), pl.BlockSpec((1, d), const)],
        out_specs=[pl.BlockSpec((tm, d), row), pl.BlockSpec((tm, d), row)],
        out_shape=[jax.ShapeDtypeStruct((m, d), F32), jax.ShapeDtypeStruct((m, d), BF16)],
        scratch_shapes=[pltpu.VMEM((tm, d), F32)],
        compiler_params=_params(("parallel", "arbitrary"), vmem),
        name="mlp_ln",
    )(xb, w1, w2, xf, g, b)


def _rglru_kernel(ag_ref, ax_ref, cw_ref, cb_ref, wa_ref, wx_ref, ba_ref, bx_ref, lam_ref,
                  o_ref, xs_ref, a_ref, b_ref, h_ref):
    i = pl.program_id(0)
    tm, c = ax_ref.shape
    sub = V7X_SUBLANES
    gw = wa_ref.shape[1]

    @pl.when(i == 0)
    def _():
        xs_ref[0:sub, :] = jnp.zeros((sub, c), F32)
        h_ref[...] = jnp.zeros_like(h_ref)

    x = ax_ref[...]
    xs_ref[sub:sub + tm, :] = x
    xc = cb_ref[...] + cw_ref[CONV_W - 1:CONV_W, :] * x
    for k in range(1, CONV_W):
        xc = xc + cw_ref[CONV_W - 1 - k:CONV_W - k, :] * xs_ref[pl.ds(sub - k, tm), :]
    xs_ref[0:sub, :] = x[tm - sub:tm, :]

    xcb = xc.astype(BF16)
    row_in_tile = lax.broadcasted_iota(jnp.int32, (tm, gw), 0) % sub
    for g in range(c // gw):
        cols = slice(g * gw, (g + 1) * gw)
        r = jax.nn.sigmoid(jnp.dot(xcb[:, cols], wa_ref[g], preferred_element_type=F32) + ba_ref[:, cols])
        ig = jax.nn.sigmoid(jnp.dot(xcb[:, cols], wx_ref[g], preferred_element_type=F32) + bx_ref[:, cols])
        log_a = (-RG_C * r) * jax.nn.softplus(-lam_ref[:, cols])
        a = jnp.exp(log_a)
        u = jnp.sqrt(-jnp.tanh(log_a) * (a * a + 1.0)) * (ig * xc[:, cols])
        for s in (1, 2, 4):
            keep = row_in_tile >= s
            u = jnp.where(keep, a * pltpu.roll(u, s, 0) + u, u)
            a = jnp.where(keep, a * pltpu.roll(a, s, 0), a)
        a_ref[:, cols] = a
        b_ref[:, cols] = u

    def tile_body(t, h):
        rows = pl.ds(pl.multiple_of(t * sub, sub), sub)
        hr = a_ref[rows, :] * h + b_ref[rows, :]
        b_ref[rows, :] = hr
        return jnp.broadcast_to(hr[sub - 1:sub, :], hr.shape)

    h_ref[...] = lax.fori_loop(0, tm // sub, tile_body, h_ref[...], unroll=4)

    gate = ag_ref[...]
    cdf = 0.5 * (1.0 + jnp.tanh(math.sqrt(2.0 / math.pi) * (gate + 0.044715 * (gate * gate * gate))))
    o_ref[...] = (gate * cdf * b_ref[...]).astype(o_ref.dtype)


def _rglru(ga, conv_w, conv_b, wa, wx, ba, bx, lam):
    m = ga.shape[0]
    c = ga.shape[1] // 2
    tm = min(RGLRU_ROWS, m)
    assert m % tm == 0
    gw = wa.shape[1]
    const2 = lambda i: (0, 0)
    vmem = 2 * (2 * _nbytes((tm, c), F32) + _nbytes((tm, c), BF16) + 2 * _nbytes(wa.shape, BF16)) \
        + 3 * _nbytes((tm + V7X_SUBLANES, c), F32) + 12 * _nbytes((tm, c), F32)
    return pl.pallas_call(
        _rglru_kernel,
        grid=(m // tm,),
        in_specs=[pl.BlockSpec((tm, c), lambda i: (i, 0)),
                  pl.BlockSpec((tm, c), lambda i: (i, 1)),
                  pl.BlockSpec((CONV_W, c), const2), pl.BlockSpec((1, c), const2),
                  pl.BlockSpec(wa.shape, lambda i: (0, 0, 0)), pl.BlockSpec(wx.shape, lambda i: (0, 0, 0)),
                  pl.BlockSpec((1, c), const2), pl.BlockSpec((1, c), const2), pl.BlockSpec((1, c), const2)],
        out_specs=pl.BlockSpec((tm, c), lambda i: (i, 0)),
        out_shape=jax.ShapeDtypeStruct((m, c), BF16),
        scratch_shapes=[pltpu.VMEM((tm + V7X_SUBLANES, c), F32), pltpu.VMEM((tm, c), F32),
                        pltpu.VMEM((tm, c), F32), pltpu.VMEM((V7X_SUBLANES, c), F32)],
        compiler_params=_params(("arbitrary",), vmem),
        name="rglru",
    )(ga, ga, conv_w, conv_b, wa, wx, ba, bx, lam)


def _block_diag_groups(w, group):
    n, bw, _ = w.shape
    per = group // bw
    eye = jnp.eye(per, dtype=w.dtype)
    wg = w.reshape(n // per, per, bw, bw)
    return jnp.einsum('gpab,pq->gpaqb', wg, eye).reshape(n // per, group, group)


def _dil_kernel(*refs, has_prev, regroup):
    q_ref, kc_ref, kp_ref, vc_ref, vp_ref = refs[:5]
    refs = refs[5:]
    if has_prev:
        op_ref, lp_ref = refs[:2]
        refs = refs[2:]
    if regroup:
        o_ref, l_ref, oslab_ref, lslab_ref = refs
    else:
        (o_ref,) = refs

    i = pl.program_id(1)
    tq = q_ref.shape[0]
    n_heads = q_ref.shape[1] // HEAD_DIM
    qi = lax.broadcasted_iota(jnp.int32, (BLK, 2 * BLK), 0)
    kj = lax.broadcasted_iota(jnp.int32, (BLK, 2 * BLK), 1)
    band = (kj >= qi) & (kj <= qi + BLK)
    lane = lax.broadcasted_iota(jnp.int32, (BLK, HEAD_DIM), 1)
    contract_last = (((1,), (1,)), ((), ()))
    for b in range(tq // BLK):
        rows = slice(b * BLK, (b + 1) * BLK)
        valid = band & (kj >= jnp.where(i > 0, 0, BLK)) if b == 0 else band
        bias = jnp.where(valid, 0.0, NEG)
        lp_all = lp_ref[rows, :] if has_prev else None
        lse_all = jnp.zeros((BLK, HEAD_DIM), F32)
        for h in range(n_heads):
            cols = slice(h * HEAD_DIM, (h + 1) * HEAD_DIM)
            if b == 0:
                kk = jnp.concatenate([kp_ref[:, cols], kc_ref[0:BLK, cols]], axis=0)
                vv = jnp.concatenate([vp_ref[:, cols], vc_ref[0:BLK, cols]], axis=0)
            else:
                kk = kc_ref[(b - 1) * BLK:(b + 1) * BLK, cols]
                vv = vc_ref[(b - 1) * BLK:(b + 1) * BLK, cols]
            s = lax.dot_general(q_ref[rows, cols], kk, contract_last, preferred_element_type=F32) + bias
            m = jnp.max(s, axis=-1, keepdims=True)
            p = jnp.exp2(s - m)
            l = jnp.sum(p, axis=-1, keepdims=True)
            o = jnp.dot(p.astype(BF16), vv, preferred_element_type=F32) / l
            lse = m + jnp.log2(l)
            if has_prev:
                lp = jnp.sum(jnp.where(lane == h, lp_all, 0.0), axis=-1, keepdims=True)
                mx = jnp.maximum(lp, lse)
                wa = jnp.exp2(lp - mx)
                wb = jnp.exp2(lse - mx)
                tot = wa + wb
                o = (op_ref[rows, cols] * wa + o * wb) / tot
                lse = mx + jnp.log2(tot)
            if regroup:
                lse_all = jnp.where(lane == h, lse, lse_all)
                base = (b * n_heads + h) * BLK
                oslab_ref[base:base + BLK, :] = o
                part = BLK // regroup
                for c in range(regroup):
                    o_ref[c, b * part:(b + 1) * part, cols] = oslab_ref[pl.ds(base + c, part, stride=regroup), :]
            else:
                o_ref[rows, cols] = o.astype(o_ref.dtype)
        if regroup:
            lslab_ref[b * BLK:(b + 1) * BLK, :] = lse_all
            for c in range(regroup):
                l_ref[c, b * part:(b + 1) * part, :] = lslab_ref[pl.ds(b * BLK + c, part, stride=regroup), :]


def _dilated_attention(qkv_by_dil):
    dils = [dil for _, dil in DIL_PATTERNS]
    s_len = qkv_by_dil[dils[0]].shape[1] * dils[0]
    c = qkv_by_dil[dils[0]].shape[2] // 3
    o = lse = None
    for idx, (window, dil) in enumerate(DIL_PATTERNS):
        assert window // dil == BLK
        has_prev, is_last = idx > 0, idx == len(DIL_PATTERNS) - 1
        regroup = 0 if is_last else dils[idx + 1] // dil
        length = s_len // dil
        tq = min(DIL_ROWS, length)
        assert length % tq == 0 and tq % BLK == 0
        per = tq // BLK
        cur = lambda off: pl.BlockSpec((None, tq, c), lambda r, i, off=off: (r, i, off))
        prev = lambda off: pl.BlockSpec((None, BLK, c),
                                        lambda r, i, off=off: (r, jnp.maximum(i * per - 1, 0), off))
        in_specs = [cur(0), cur(1), prev(1), cur(2), prev(2)]
        args = [qkv_by_dil[dil]] * 5
        if has_prev:
            in_specs += [cur(0), pl.BlockSpec((None, tq, HEAD_DIM), lambda r, i: (r, i, 0))]
            args += [o, lse]
        scratch = []
        if is_last:
            out_specs = [pl.BlockSpec((None, tq, c), lambda r, i: (r, i, 0))]
            out_shape = [jax.ShapeDtypeStruct((dil, length, c), BF16)]
        else:
            assert dils[idx + 1] == regroup * dil and BLK % (regroup * V7X_SUBLANES) == 0
            grouped = lambda w: pl.BlockSpec((regroup, None, tq // regroup, w), lambda r, i: (0, r, i, 0))
            out_specs = [grouped(c), grouped(HEAD_DIM)]
            out_shape = [jax.ShapeDtypeStruct((regroup, dil, length // regroup, w), F32) for w in (c, HEAD_DIM)]
            scratch = [pltpu.VMEM((per * (c // HEAD_DIM) * BLK, HEAD_DIM), F32), pltpu.VMEM((tq, HEAD_DIM), F32)]
        vmem = 2 * (3 * _nbytes((tq, c), BF16) + 2 * _nbytes((BLK, c), BF16) + 4 * _nbytes((tq, c), F32)) \
            + 2 * _nbytes((tq, c), F32) + 16 * _nbytes((BLK, 2 * BLK), F32) * (c // HEAD_DIM)
        res = pl.pallas_call(
            functools.partial(_dil_kernel, has_prev=has_prev, regroup=regroup),
            grid=(dil, length // tq),
            in_specs=in_specs,
            out_specs=out_specs,
            out_shape=out_shape,
            scratch_shapes=scratch,
            compiler_params=_params(("parallel", "parallel"), vmem),
            name=f"dilated_attn_d{dil}",
        )(*args)
        if is_last:
            return res[0].transpose(1, 0, 2).reshape(s_len, c)
        o, lse = (r.reshape(regroup * dil, length // regroup, r.shape[-1]) for r in res)


def _diff_attn_kernel(q_ref, k_ref, vt_ref, lq1_ref, lk1_ref, lq2_ref, lk2_ref, g_ref, o_ref,
                      m_ref, l_ref, acc_ref, s_ref, smax_ref, *, lambda_init, tk):
    tq = q_ref.shape[0]
    assert tk % tq == 0
    q0 = pl.program_id(1) * tq
    jd = q0 // tk
    m_ref[...] = jnp.full_like(m_ref, NEG)
    l_ref[...] = jnp.zeros_like(l_ref)
    acc_ref[...] = jnp.zeros_like(acc_ref)
    contract_last = (((1,), (1,)), ((), ()))

    def scores(j, slot):
        kb = k_ref[pl.ds(pl.multiple_of(j * tk, tk), tk), :]
        for c in range(2):
            cols = slice(c * HEAD_DIM, (c + 1) * HEAD_DIM)
            s = lax.dot_general(kb[:, cols], q_ref[:, cols], contract_last, preferred_element_type=F32)
            s_ref[slot, c] = s
            smax_ref[slot, c] = jnp.max(s, axis=0, keepdims=True)

    def update(j, slot, masked):
        vt = vt_ref[j]
        for c in range(2):
            s = s_ref[slot, c]
            if masked:
                kj = lax.broadcasted_iota(jnp.int32, (tk, tq), 0)
                qi = lax.broadcasted_iota(jnp.int32, (tk, tq), 1)
                s = jnp.where(kj <= qi + (q0 - jd * tk), s, NEG)
                m_cur = jnp.max(s, axis=0, keepdims=True)
            else:
                m_cur = smax_ref[slot, c]
            m_prev = m_ref[c]
            m_new = jnp.maximum(m_prev, m_cur)
            alpha = jnp.exp2(m_prev - m_new)
            p = jnp.exp2(s - m_new)
            l_ref[c] = alpha * l_ref[c] + jnp.sum(p, axis=0, keepdims=True)
            m_ref[c] = m_new
            pv = jnp.dot(vt, p.astype(BF16), preferred_element_type=F32)
            acc_ref[c] = acc_ref[c] * alpha + pv

    scores(0, 0)

    def pair(p, carry):
        scores(2 * p + 1, 1)
        update(2 * p, 0, False)
        scores(2 * p + 2, 0)
        update(2 * p + 1, 1, False)
        return carry

    lax.fori_loop(0, jd // 2, pair, 0)

    @pl.when(jd % 2 == 0)
    def _():
        update(jd, 0, True)

    @pl.when(jd % 2 == 1)
    def _():
        scores(jd, 1)
        update(jd - 1, 0, False)
        update(jd, 1, True)

    lam = (jnp.exp(jnp.sum(lq1_ref[...] * lk1_ref[...], axis=-1, keepdims=True))
           - jnp.exp(jnp.sum(lq2_ref[...] * lk2_ref[...], axis=-1, keepdims=True)) + lambda_init)
    o = acc_ref[0] * (1.0 / l_ref[0]) - lam * (acc_ref[1] * (1.0 / l_ref[1]))
    g = jnp.concatenate([g_ref[...]] * (tq // HEAD_DIM), axis=1)
    o = o * lax.rsqrt(jnp.mean(o * o, axis=0, keepdims=True) + RMS_EPS) * g
    o_ref[...] = (o * (1.0 - lambda_init)).T.astype(o_ref.dtype)


def _diff_attention(qkv, lq1, lk1, lq2, lk2, g, lambda_init):
    s_len = qkv.shape[0]
    d = qkv.shape[1] // 3
    hw = 2 * HEAD_DIM
    n_heads = d // hw
    tq = tk = min(ATTN_TILE, s_len)
    assert s_len % tk == 0 and tk % tq == 0
    n_kv = s_len // tk
    vt = qkv[:, 2 * d:].reshape(n_kv, tk, n_heads, hw).transpose(2, 0, 3, 1)
    g_cols = jnp.broadcast_to(g.reshape(hw, 1), (hw, HEAD_DIM))
    vec = pl.BlockSpec((1, HEAD_DIM), lambda h, i: (0, 0))
    kv_mode = pl.Buffered(1)
    vmem = 2 * _nbytes((s_len, hw), BF16) + 4 * _nbytes((tq, hw), BF16) \
        + 2 * (4 * _nbytes((V7X_SUBLANES, tq), F32) + _nbytes((tq, hw), F32)) + 16 * _nbytes((tq, tk), F32)
    return pl.pallas_call(
        functools.partial(_diff_attn_kernel, lambda_init=lambda_init, tk=tk),
        grid=(n_heads, s_len // tq),
        in_specs=[pl.BlockSpec((tq, hw), lambda h, i: (i, h)),
                  pl.BlockSpec((s_len, hw), lambda h, i: (0, n_heads + h), pipeline_mode=kv_mode),
                  pl.BlockSpec((None, n_kv, hw, tk), lambda h, i: (h, 0, 0, 0), pipeline_mode=kv_mode),
                  vec, vec, vec, vec,
                  pl.BlockSpec((hw, HEAD_DIM), lambda h, i: (0, 0))],
        out_specs=pl.BlockSpec((tq, hw), lambda h, i: (i, h)),
        out_shape=jax.ShapeDtypeStruct((s_len, d), BF16),
        scratch_shapes=[pltpu.VMEM((2, 1, tq), F32), pltpu.VMEM((2, 1, tq), F32),
                        pltpu.VMEM((2, hw, tq), F32), pltpu.VMEM((2, 2, tk, tq), F32),
                        pltpu.VMEM((2, 2, 1, tq), F32)],
        compiler_params=_params(("parallel", "parallel"), vmem),
        name="diff_attn",
    )(qkv, qkv, vt, lq1, lk1, lq2, lk2, g_cols)


def _rope_tables(positions, q_scale):
    inv = ROPE_THETA ** (-jnp.arange(0, ROT_DIM, 2, dtype=F32) / ROT_DIM)
    ang = positions.astype(F32)[:, None] * inv
    cos, sin = jnp.cos(ang), jnp.sin(ang)
    n = positions.shape[0]
    rest = HEAD_DIM - ROT_DIM
    c = jnp.concatenate([cos, cos, jnp.ones((n, rest), F32)], axis=1)
    s1 = jnp.concatenate([jnp.zeros((n, ROT_HALF), F32), sin, jnp.zeros((n, rest), F32)], axis=1)
    s2 = jnp.concatenate([-sin, jnp.zeros((n, ROT_HALF + rest), F32)], axis=1)
    zero = jnp.zeros_like(c)
    return (jnp.stack([c * q_scale, c, jnp.ones_like(c)]),
            jnp.stack([s1 * q_scale, s1, zero]),
            jnp.stack([s2 * q_scale, s2, zero]))


def kernel(x, positions, ev_w_in, ev_conv_w, ev_conv_b, ev_gate_a_w, ev_gate_a_b, ev_gate_x_w, ev_gate_x_b, ev_rg_lambda, ev_w_out, od_w_in, od_lambda_q1, od_lambda_k1, od_lambda_q2, od_lambda_k2, od_subln_g, od_w_out, ln_mix_g, ln_mix_b, ln_mlp_g, ln_mlp_b, mlp_w1, mlp_w2):
    batch, s_len, d = x.shape
    depth = ln_mix_g.shape[0]
    d_rnn = ev_conv_w.shape[-1]
    d_dil = (ev_w_in.shape[-1] - 2 * d_rnn) // 3
    alpha = (2 * depth) ** 0.25
    q_scale = LOG2E * HEAD_DIM ** -0.5
    row = lambda v: v.reshape(1, -1)

    ev_w_in_b, ev_w_out_b = ev_w_in.astype(BF16), ev_w_out.astype(BF16)
    od_w_in_b, od_w_out_b = od_w_in.astype(BF16), od_w_out.astype(BF16)
    mlp_w1_b, mlp_w2_b = mlp_w1.astype(BF16), mlp_w2.astype(BF16)
    dils = [dil for _, dil in DIL_PATTERNS]

    outs = []
    for bi in range(batch):
        xf = x[bi]
        xb = xf.astype(BF16)
        tabs = _rope_tables(positions[bi], q_scale)
        for layer in range(depth):
            p = layer // 2
            if layer % 2 == 0:
                ga = _proj(xb, ev_w_in_b, p, 0, 2 * d_rnn, F32)
                qkvs = _proj_rope(xb, ev_w_in_b, p, 2 * d_rnn, 3 * d_dil, tabs, d_dil, 2 * d_dil,
                                  [dil for dil in dils if dil > 1])
                qkv_by_dil = dict(zip(dils, [qkvs[0][None]] + list(qkvs[1:])))
                wa = _block_diag_groups(ev_gate_a_w[p], V7X_MXU_DIM).astype(BF16)
                wx = _block_diag_groups(ev_gate_x_w[p], V7X_MXU_DIM).astype(BF16)
                ya = _rglru(ga, ev_conv_w[p], row(ev_conv_b[p]), wa, wx, row(ev_gate_a_b[p]),
                            row(ev_gate_x_b[p]), row(ev_rg_lambda[p]))
                yb = _dilated_attention(qkv_by_dil)
                ys, w_out = [ya, yb], ev_w_out_b
            else:
                lambda_init = 0.8 - 0.6 * math.exp(-0.3 * layer)
                (qkv,) = _proj_rope(xb, od_w_in_b, p, 0, 3 * d, tabs, d, 2 * d)
                y = _diff_attention(qkv, row(od_lambda_q1[p]), row(od_lambda_k1[p]), row(od_lambda_q2[p]),
                                    row(od_lambda_k2[p]), row(od_subln_g[p]), lambda_init)
                ys, w_out = [y], od_w_out_b
            xf, xb = _outproj_ln(ys, w_out, p, xf, row(ln_mix_g[layer]), row(ln_mix_b[layer]), alpha)
            xf, xb = _mlp_ln(xb, mlp_w1_b, mlp_w2_b, layer, xf, row(ln_mlp_g[layer]), row(ln_mlp_b[layer]), alpha)
        outs.append(xf)
    return outs[0][None] if batch == 1 else jnp.stack(outs, axis=0)
```

```python
import functools
import math

import jax
import jax.numpy as jnp
from jax import lax
from jax.experimental import pallas as pl
from jax.experimental.pallas import tpu as pltpu

HEAD_DIM = 128
ROT_DIM = HEAD_DIM // 4
ROT_HALF = ROT_DIM // 2
ROPE_THETA = 500000.0
CONV_W = 4
RG_C = 8.0
RNN_BLOCK_W = 64
DIL_PATTERNS = ((128, 1), (512, 4), (2048, 16))
BLK = 128
LN_EPS = 1e-5
RMS_EPS = 1e-5
NEG = -1e30
LOG2E = 1.4426950408889634

V7X_VMEM_BYTES = 64 * 2**20
V7X_VMEM_RESERVED_BYTES = 8 * 2**20
V7X_SUBLANES = 8
V7X_MXU_DIM = 256

PROJ_ROWS = 1024
PROJ_COLS = 1024
PROJ_COLS_WIDE = 2048
OUTPROJ_ROWS = 512
MLP_ROWS = 512
MLP_HIDDEN_COLS = 1024
LN_SUB_ROWS = V7X_MXU_DIM
RGLRU_ROWS = 256
DIL_ROWS = 512
ATTN_TILE = 1024

F32 = jnp.float32
BF16 = jnp.bfloat16


def _params(semantics, vmem_bytes):
    limit = min(int(vmem_bytes), V7X_VMEM_BYTES - V7X_VMEM_RESERVED_BYTES)
    return pltpu.CompilerParams(dimension_semantics=semantics, vmem_limit_bytes=limit)


def _nbytes(shape, dtype):
    return math.prod(shape) * jnp.dtype(dtype).itemsize


def _ln_rows(z, g, b):
    mu = jnp.mean(z, axis=-1, keepdims=True)
    zc = z - mu
    var = jnp.mean(zc * zc, axis=-1, keepdims=True)
    return zc * lax.rsqrt(var + LN_EPS) * g + b


def _proj_kernel(x_ref, w_ref, o_ref):
    o_ref[...] = jnp.dot(x_ref[...], w_ref[...], preferred_element_type=F32).astype(o_ref.dtype)


def _proj_rope_kernel(x_ref, w_ref, c_ref, s1_ref, s2_ref, o_ref, *rest, dilations):
    dil_refs, slab_refs = rest[:len(dilations)], rest[len(dilations):]
    tm = x_ref.shape[0]
    acc = jnp.dot(x_ref[...], w_ref[...], preferred_element_type=F32)
    c, s1, s2 = c_ref[...], s1_ref[...], s2_ref[...]
    for h in range(acc.shape[1] // HEAD_DIM):
        cols = slice(h * HEAD_DIM, (h + 1) * HEAD_DIM)
        a = acc[:, cols]
        r = a * c + pltpu.roll(a, ROT_HALF, 1) * s1 + pltpu.roll(a, HEAD_DIM - ROT_HALF, 1) * s2
        o_ref[:, cols] = r.astype(o_ref.dtype)
        if dilations:
            base, src_ref, src_d = h * tm, slab_refs[0], 1
            src_ref[base:base + tm, :] = r
            for idx, (d, d_ref) in enumerate(zip(dilations, dil_refs)):
                f, n_rows = d // src_d, tm // d
                dst_ref = slab_refs[(idx + 1) % 2]
                for rho_src in range(src_d):
                    for k in range(f):
                        rho = k * src_d + rho_src
                        val = src_ref[pl.ds(base + rho_src * (tm // src_d) + k, n_rows, stride=f), :]
                        d_ref[rho, :, cols] = val.astype(d_ref.dtype)
                        if idx + 1 < len(dilations):
                            dst_ref[base + rho * n_rows:base + (rho + 1) * n_rows, :] = val
                src_ref, src_d = dst_ref, d


def _proj_tiles(m, n, wide):
    tm = min(PROJ_ROWS, m)
    tn = min(PROJ_COLS_WIDE if wide else PROJ_COLS, n)
    assert m % tm == 0 and n % tn == 0
    return tm, tn


def _proj(xb, w, layer, col0, n, out_dtype):
    m, k = xb.shape
    tm, tn = _proj_tiles(m, n, wide=True)
    assert col0 % tn == 0
    vmem = 2 * (_nbytes((tm, k), BF16) + _nbytes((k, tn), BF16) + _nbytes((tm, tn), out_dtype)) \
        + 2 * _nbytes((tm, tn), F32)
    return pl.pallas_call(
        _proj_kernel,
        grid=(m // tm, n // tn),
        in_specs=[pl.BlockSpec((tm, k), lambda i, j: (i, 0)),
                  pl.BlockSpec((None, k, tn), lambda i, j: (layer, 0, j + col0 // tn))],
        out_specs=pl.BlockSpec((tm, tn), lambda i, j: (i, j)),
        out_shape=jax.ShapeDtypeStruct((m, n), out_dtype),
        compiler_params=_params(("parallel", "arbitrary"), vmem),
        name="proj",
    )(xb, w)


def _proj_rope(xb, w, layer, col0, n, rope_tabs, n_q_cols, n_rope_cols, dilations=()):
    m, k = xb.shape
    tm, tn = _proj_tiles(m, n, wide=not dilations)
    assert n_q_cols % tn == 0 and n_rope_cols % tn == 0 and col0 % tn == 0
    n_q, n_rope = n_q_cols // tn, n_rope_cols // tn
    variant = lambda j: jnp.where(j < n_q, 0, jnp.where(j < n_rope, 1, 2))
    tab_spec = pl.BlockSpec((None, tm, HEAD_DIM), lambda i, j: (variant(j), i, 0))
    out_specs = [pl.BlockSpec((tm, tn), lambda i, j: (i, j))]
    out_shape = [jax.ShapeDtypeStruct((m, n), BF16)]
    for prev_d, d in zip((1,) + tuple(dilations), dilations):
        assert d % prev_d == 0 and tm % (d * 2 * V7X_SUBLANES) == 0
        out_specs.append(pl.BlockSpec((d, tm // d, tn), lambda i, j: (0, i, j)))
        out_shape.append(jax.ShapeDtypeStruct((d, m // d, n), BF16))
    scratch = [pltpu.VMEM((tn // HEAD_DIM * tm, HEAD_DIM), F32)] * (2 if dilations else 0)
    vmem = 2 * (_nbytes((tm, k), BF16) + _nbytes((k, tn), BF16) + (1 + len(dilations)) * _nbytes((tm, tn), BF16)
                + 3 * _nbytes((tm, HEAD_DIM), F32)) + (3 + 2 * bool(dilations)) * _nbytes((tm, tn), F32)
    return pl.pallas_call(
        functools.partial(_proj_rope_kernel, dilations=tuple(dilations)),
        grid=(m // tm, n // tn),
        in_specs=[pl.BlockSpec((tm, k), lambda i, j: (i, 0)),
                  pl.BlockSpec((None, k, tn), lambda i, j: (layer, 0, j + col0 // tn)),
                  tab_spec, tab_spec, tab_spec],
        out_specs=out_specs,
        out_shape=out_shape,
        scratch_shapes=scratch,
        compiler_params=_params(("parallel", "arbitrary"), vmem),
        name="proj_rope",
    )(xb, w, *rope_tabs)


def _outproj_ln_kernel(*refs, n_in, alpha, sub):
    ys, ws = refs[:n_in], refs[n_in:2 * n_in]
    x_ref, g_ref, b_ref, of_ref, ob_ref = refs[2 * n_in:]
    for r in range(x_ref.shape[0] // sub):
        rows = slice(r * sub, (r + 1) * sub)
        acc = jnp.dot(ys[0][rows, :], ws[0][...], preferred_element_type=F32)
        for y_ref, w_ref in zip(ys[1:], ws[1:]):
            acc = acc + jnp.dot(y_ref[rows, :], w_ref[...], preferred_element_type=F32)
        y = _ln_rows(alpha * x_ref[rows, :] + acc, g_ref[...], b_ref[...])
        of_ref[rows, :] = y
        ob_ref[rows, :] = y.astype(BF16)


def _outproj_ln(ys, w, layer, xf, g, b, alpha):
    m, d = xf.shape
    tm = min(OUTPROJ_ROWS, m)
    assert m % tm == 0
    row = lambda i: (i, 0)
    const = lambda i: (0, 0)
    in_specs = [pl.BlockSpec((tm, y.shape[1]), row) for y in ys]
    row0 = 0
    for y in ys:
        rows = y.shape[1]
        assert row0 % rows == 0
        in_specs.append(pl.BlockSpec((None, rows, d), lambda i, blk=row0 // rows: (layer, blk, 0)))
        row0 += rows
    assert row0 == w.shape[1]
    in_specs += [pl.BlockSpec((tm, d), row), pl.BlockSpec((1, d), const), pl.BlockSpec((1, d), const)]
    vmem = 2 * (sum(_nbytes((tm, y.shape[1]), BF16) for y in ys) + _nbytes(w.shape[1:], BF16)
                + 2 * _nbytes((tm, d), F32) + _nbytes((tm, d), BF16)) + 3 * _nbytes((tm, d), F32)
    return pl.pallas_call(
        functools.partial(_outproj_ln_kernel, n_in=len(ys), alpha=alpha, sub=min(LN_SUB_ROWS, tm)),
        grid=(m // tm,),
        in_specs=in_specs,
        out_specs=[pl.BlockSpec((tm, d), row), pl.BlockSpec((tm, d), row)],
        out_shape=[jax.ShapeDtypeStruct((m, d), F32), jax.ShapeDtypeStruct((m, d), BF16)],
        compiler_params=_params(("parallel",), vmem),
        name="outproj_ln",
    )(*ys, *([w] * len(ys)), xf, g, b)


def _mlp_ln_kernel(xb_ref, w1_ref, w2_ref, x_ref, g_ref, b_ref, of_ref, ob_ref, acc_ref, *, alpha, sub):
    j = pl.program_id(1)
    last = pl.num_programs(1) - 1

    def hidden():
        h = jnp.maximum(jnp.dot(xb_ref[...], w1_ref[...], preferred_element_type=F32), 0.0)
        return (h * h).astype(BF16)

    @pl.when(j == 0)
    def _():
        acc_ref[...] = jnp.zeros_like(acc_ref)

    @pl.when(j < last)
    def _():
        acc_ref[...] += jnp.dot(hidden(), w2_ref[...], preferred_element_type=F32)

    @pl.when(j == last)
    def _():
        h = hidden()
        for r in range(x_ref.shape[0] // sub):
            rows = slice(r * sub, (r + 1) * sub)
            z = acc_ref[rows, :] + jnp.dot(h[rows, :], w2_ref[...], preferred_element_type=F32)
            y = _ln_rows(alpha * x_ref[rows, :] + z, g_ref[...], b_ref[...])
            of_ref[rows, :] = y
            ob_ref[rows, :] = y.astype(BF16)


def _mlp_ln(xb, w1, w2, layer, xf, g, b, alpha):
    m, d = xf.shape
    f = w1.shape[2]
    tm = min(MLP_ROWS, m)
    tf = min(MLP_HIDDEN_COLS, f)
    assert m % tm == 0 and f % tf == 0
    row = lambda i, j: (i, 0)
    const = lambda i, j: (0, 0)
    vmem = 2 * (_nbytes((tm, d), BF16) + 2 * _nbytes((d, tf), BF16) + 2 * _nbytes((tm, d), F32)
                + _nbytes((tm, d), BF16)) + 2 * _nbytes((tm, d), F32) + 4 * _nbytes((tm, tf), F32)
    return pl.pallas_call(
        functools.partial(_mlp_ln_kernel, alpha=alpha, sub=min(LN_SUB_ROWS, tm)),
        grid=(m // tm, f // tf),
        in_specs=[pl.BlockSpec((tm, d), row),
                  pl.BlockSpec((None, d, tf), lambda i, j: (layer, 0, j)),
                  pl.BlockSpec((None, tf, d), lambda i, j: (layer, j, 0)),
                  pl.BlockSpec((tm, d), row),
                  pl.BlockSpec((1, d), const), pl.BlockSpec((1, d), const)],
        out_specs=[pl.BlockSpec((tm, d), row), pl.BlockSpec((tm, d), row)],
        out_shape=[jax.ShapeDtypeStruct((m, d), F32), jax.ShapeDtypeStruct((m, d), BF16)],
        scratch_shapes=[pltpu.VMEM((tm, d), F32)],
        compiler_params=_params(("parallel", "arbitrary"), vmem),
        name="mlp_ln",
    )(xb, w1, w2, xf, g, b)


def _rglru_kernel(ag_ref, ax_ref, cw_ref, cb_ref, wa_ref, wx_ref, ba_ref, bx_ref, lam_ref,
                  o_ref, xs_ref, a_ref, b_ref, h_ref):
    i = pl.program_id(0)
    tm, c = ax_ref.shape
    sub = V7X_SUBLANES
    gw = wa_ref.shape[1]

    @pl.when(i == 0)
    def _():
        xs_ref[0:sub, :] = jnp.zeros((sub, c), F32)
        h_ref[...] = jnp.zeros_like(h_ref)

    x = ax_ref[...]
    xs_ref[sub:sub + tm, :] = x
    xc = cb_ref[...] + cw_ref[CONV_W - 1:CONV_W, :] * x
    for k in range(1, CONV_W):
        xc = xc + cw_ref[CONV_W - 1 - k:CONV_W - k, :] * xs_ref[pl.ds(sub - k, tm), :]
    xs_ref[0:sub, :] = x[tm - sub:tm, :]

    xcb = xc.astype(BF16)
    row_in_tile = lax.broadcasted_iota(jnp.int32, (tm, gw), 0) % sub
    for g in range(c // gw):
        cols = slice(g * gw, (g + 1) * gw)
        r = jax.nn.sigmoid(jnp.dot(xcb[:, cols], wa_ref[g], preferred_element_type=F32) + ba_ref[:, cols])
        ig = jax.nn.sigmoid(jnp.dot(xcb[:, cols], wx_ref[g], preferred_element_type=F32) + bx_ref[:, cols])
        log_a = (-RG_C * r) * jax.nn.softplus(-lam_ref[:, cols])
        a = jnp.exp(log_a)
        u = jnp.sqrt(-jnp.tanh(log_a) * (a * a + 1.0)) * (ig * xc[:, cols])
        for s in (1, 2, 4):
            keep = row_in_tile >= s
            u = jnp.where(keep, a * pltpu.roll(u, s, 0) + u, u)
            a = jnp.where(keep, a * pltpu.roll(a, s, 0), a)
        a_ref[:, cols] = a
        b_ref[:, cols] = u

    def tile_body(t, h):
        rows = pl.ds(pl.multiple_of(t * sub, sub), sub)
        hr = a_ref[rows, :] * h + b_ref[rows, :]
        b_ref[rows, :] = hr
        return jnp.broadcast_to(hr[sub - 1:sub, :], hr.shape)

    h_ref[...] = lax.fori_loop(0, tm // sub, tile_body, h_ref[...], unroll=4)

    gate = ag_ref[...]
    cdf = 0.5 * (1.0 + jnp.tanh(math.sqrt(2.0 / math.pi) * (gate + 0.044715 * (gate * gate * gate))))
    o_ref[...] = (gate * cdf * b_ref[...]).astype(o_ref.dtype)


def _rglru(ga, conv_w, conv_b, wa, wx, ba, bx, lam):
    m = ga.shape[0]
    c = ga.shape[1] // 2
    tm = min(RGLRU_ROWS, m)
    assert m % tm == 0
    gw = wa.shape[1]
    const2 = lambda i: (0, 0)
    vmem = 2 * (2 * _nbytes((tm, c), F32) + _nbytes((tm, c), BF16) + 2 * _nbytes(wa.shape, BF16)) \
        + 3 * _nbytes((tm + V7X_SUBLANES, c), F32) + 12 * _nbytes((tm, c), F32)
    return pl.pallas_call(
        _rglru_kernel,
        grid=(m // tm,),
        in_specs=[pl.BlockSpec((tm, c), lambda i: (i, 0)),
                  pl.BlockSpec((tm, c), lambda i: (i, 1)),
                  pl.BlockSpec((CONV_W, c), const2), pl.BlockSpec((1, c), const2),
                  pl.BlockSpec(wa.shape, lambda i: (0, 0, 0)), pl.BlockSpec(wx.shape, lambda i: (0, 0, 0)),
                  pl.BlockSpec((1, c), const2), pl.BlockSpec((1, c), const2), pl.BlockSpec((1, c), const2)],
        out_specs=pl.BlockSpec((tm, c), lambda i: (i, 0)),
        out_shape=jax.ShapeDtypeStruct((m, c), BF16),
        scratch_shapes=[pltpu.VMEM((tm + V7X_SUBLANES, c), F32), pltpu.VMEM((tm, c), F32),
                        pltpu.VMEM((tm, c), F32), pltpu.VMEM((V7X_SUBLANES, c), F32)],
        compiler_params=_params(("arbitrary",), vmem),
        name="rglru",
    )(ga, ga, conv_w, conv_b, wa, wx, ba, bx, lam)


def _block_diag_groups(w, group):
    n, bw, _ = w.shape
    per = group // bw
    eye = jnp.eye(per, dtype=w.dtype)
    wg = w.reshape(n // per, per, bw, bw)
    return jnp.einsum('gpab,pq->gpaqb', wg, eye).reshape(n // per, group, group)


def _dil_kernel(*refs, has_prev, regroup):
    q_ref, kc_ref, kp_ref, vc_ref, vp_ref = refs[:5]
    refs = refs[5:]
    if has_prev:
        op_ref, lp_ref = refs[:2]
        refs = refs[2:]
    if regroup:
        o_ref, l_ref, oslab_ref, lslab_ref = refs
    else:
        (o_ref,) = refs

    i = pl.program_id(1)
    tq = q_ref.shape[0]
    n_heads = q_ref.shape[1] // HEAD_DIM
    qi = lax.broadcasted_iota(jnp.int32, (BLK, 2 * BLK), 0)
    kj = lax.broadcasted_iota(jnp.int32, (BLK, 2 * BLK), 1)
    band = (kj >= qi) & (kj <= qi + BLK)
    lane = lax.broadcasted_iota(jnp.int32, (BLK, HEAD_DIM), 1)
    contract_last = (((1,), (1,)), ((), ()))
    for b in range(tq // BLK):
        rows = slice(b * BLK, (b + 1) * BLK)
        valid = band & (kj >= jnp.where(i > 0, 0, BLK)) if b == 0 else band
        bias = jnp.where(valid, 0.0, NEG)
        lp_all = lp_ref[rows, :] if has_prev else None
        lse_all = jnp.zeros((BLK, HEAD_DIM), F32)
        for h in range(n_heads):
            cols = slice(h * HEAD_DIM, (h + 1) * HEAD_DIM)
            if b == 0:
                kk = jnp.concatenate([kp_ref[:, cols], kc_ref[0:BLK, cols]], axis=0)
                vv = jnp.concatenate([vp_ref[:, cols], vc_ref[0:BLK, cols]], axis=0)
            else:
                kk = kc_ref[(b - 1) * BLK:(b + 1) * BLK, cols]
                vv = vc_ref[(b - 1) * BLK:(b + 1) * BLK, cols]
            s = lax.dot_general(q_ref[rows, cols], kk, contract_last, preferred_element_type=F32) + bias
            m = jnp.max(s, axis=-1, keepdims=True)
            p = jnp.exp2(s - m)
            l = jnp.sum(p, axis=-1, keepdims=True)
            o = jnp.dot(p.astype(BF16), vv, preferred_element_type=F32) / l
            lse = m + jnp.log2(l)
            if has_prev:
                lp = jnp.sum(jnp.where(lane == h, lp_all, 0.0), axis=-1, keepdims=True)
                mx = jnp.maximum(lp, lse)
                wa = jnp.exp2(lp - mx)
                wb = jnp.exp2(lse - mx)
                tot = wa + wb
                o = (op_ref[rows, cols] * wa + o * wb) / tot
                lse = mx + jnp.log2(tot)
            if regroup:
                lse_all = jnp.where(lane == h, lse, lse_all)
                base = (b * n_heads + h) * BLK
                oslab_ref[base:base + BLK, :] = o
                part = BLK // regroup
                for c in range(regroup):
                    o_ref[c, b * part:(b + 1) * part, cols] = oslab_ref[pl.ds(base + c, part, stride=regroup), :]
            else:
                o_ref[rows, cols] = o.astype(o_ref.dtype)
        if regroup:
            lslab_ref[b * BLK:(b + 1) * BLK, :] = lse_all
            for c in range(regroup):
                l_ref[c, b * part:(b + 1) * part, :] = lslab_ref[pl.ds(b * BLK + c, part, stride=regroup), :]


def _dilated_attention(qkv_by_dil):
    dils = [dil for _, dil in DIL_PATTERNS]
    s_len = qkv_by_dil[dils[0]].shape[1] * dils[0]
    c = qkv_by_dil[dils[0]].shape[2] // 3
    o = lse = None
    for idx, (window, dil) in enumerate(DIL_PATTERNS):
        assert window // dil == BLK
        has_prev, is_last = idx > 0, idx == len(DIL_PATTERNS) - 1
        regroup = 0 if is_last else dils[idx + 1] // dil
        length = s_len // dil
        tq = min(DIL_ROWS, length)
        assert length % tq == 0 and tq % BLK == 0
        per = tq // BLK
        cur = lambda off: pl.BlockSpec((None, tq, c), lambda r, i, off=off: (r, i, off))
        prev = lambda off: pl.BlockSpec((None, BLK, c),
                                        lambda r, i, off=off: (r, jnp.maximum(i * per - 1, 0), off))
        in_specs = [cur(0), cur(1), prev(1), cur(2), prev(2)]
        args = [qkv_by_dil[dil]] * 5
        if has_prev:
            in_specs += [cur(0), pl.BlockSpec((None, tq, HEAD_DIM), lambda r, i: (r, i, 0))]
            args += [o, lse]
        scratch = []
        if is_last:
            out_specs = [pl.BlockSpec((None, tq, c), lambda r, i: (r, i, 0))]
            out_shape = [jax.ShapeDtypeStruct((dil, length, c), BF16)]
        else:
            assert dils[idx + 1] == regroup * dil and BLK % (regroup * V7X_SUBLANES) == 0
            grouped = lambda w: pl.BlockSpec((regroup, None, tq // regroup, w), lambda r, i: (0, r, i, 0))
            out_specs = [grouped(c), grouped(HEAD_DIM)]
            out_shape = [jax.ShapeDtypeStruct((regroup, dil, length // regroup, w), F32) for w in (c, HEAD_DIM)]
            scratch = [pltpu.VMEM((per * (c // HEAD_DIM) * BLK, HEAD_DIM), F32), pltpu.VMEM((tq, HEAD_DIM), F32)]
        vmem = 2 * (3 * _nbytes((tq, c), BF16) + 2 * _nbytes((BLK, c), BF16) + 4 * _nbytes((tq, c), F32)) \
            + 2 * _nbytes((tq, c), F32) + 16 * _nbytes((BLK, 2 * BLK), F32) * (c // HEAD_DIM)
        res = pl.pallas_call(
            functools.partial(_dil_kernel, has_prev=has_prev, regroup=regroup),
            grid=(dil, length // tq),
            in_specs=in_specs,
            out_specs=out_specs,
            out_shape=out_shape,
            scratch_shapes=scratch,
            compiler_params=_params(("parallel", "parallel"), vmem),
            name=f"dilated_attn_d{dil}",
        )(*args)
        if is_last:
            return res[0].transpose(1, 0, 2).reshape(s_len, c)
        o, lse = (r.reshape(regroup * dil, length // regroup, r.shape[-1]) for r in res)


def _diff_attn_kernel(q_ref, k_ref, vt_ref, lq1_ref, lk1_ref, lq2_ref, lk2_ref, g_ref, o_ref,
                      m_ref, l_ref, acc_ref, s_ref, smax_ref, *, lambda_init, tk):
    tq = q_ref.shape[0]
    assert tk % tq == 0
    q0 = pl.program_id(1) * tq
    jd = q0 // tk
    m_ref[...] = jnp.full_like(m_ref, NEG)
    l_ref[...] = jnp.zeros_like(l_ref)
    acc_ref[...] = jnp.zeros_like(acc_ref)
    contract_last = (((1,), (1,)), ((), ()))

    def scores(j, slot):
        kb = k_ref[pl.ds(pl.multiple_of(j * tk, tk), tk), :]
        for c in range(2):
            cols = slice(c * HEAD_DIM, (c + 1) * HEAD_DIM)
            s = lax.dot_general(kb[:, cols], q_ref[:, cols], contract_last, preferred_element_type=F32)
            s_ref[slot, c] = s
            smax_ref[slot, c] = jnp.max(s, axis=0, keepdims=True)

    def absorb(c, s, m_cur, vt, q_lo, k_lo):
        qs = slice(q_lo, tq)
        m_prev = m_ref[c, :, qs]
        m_new = jnp.maximum(m_prev, m_cur)
        alpha = jnp.exp2(m_prev - m_new)
        p = jnp.exp2(s - m_new)
        l_ref[c, :, qs] = alpha * l_ref[c, :, qs] + jnp.sum(p, axis=0, keepdims=True)
        m_ref[c, :, qs] = m_new
        pv = jnp.dot(vt[:, k_lo:k_lo + s.shape[0]], p.astype(BF16), preferred_element_type=F32)
        acc_ref[c, :, qs] = acc_ref[c, :, qs] * alpha + pv

    def update(j, slot, masked):
        vt = vt_ref[j]
        for c in range(2):
            if not masked:
                absorb(c, s_ref[slot, c], smax_ref[slot, c], vt, 0, 0)
                continue
            parts = [(0, tk // 2, 0), (tk // 2, tk, tq // 2)] if tq == tk else [(0, tk, 0)]
            for k_lo, k_hi, q_lo in parts:
                s = s_ref[slot, c, k_lo:k_hi, q_lo:tq]
                kj = lax.broadcasted_iota(jnp.int32, s.shape, 0) + k_lo
                qi = lax.broadcasted_iota(jnp.int32, s.shape, 1) + q_lo
                s = jnp.where(kj <= qi + (q0 - jd * tk), s, NEG)
                absorb(c, s, jnp.max(s, axis=0, keepdims=True), vt, q_lo, k_lo)

    scores(0, 0)

    def pair(p, carry):
        scores(2 * p + 1, 1)
        update(2 * p, 0, False)
        scores(2 * p + 2, 0)
        update(2 * p + 1, 1, False)
        return carry

    lax.fori_loop(0, jd // 2, pair, 0)

    @pl.when(jd % 2 == 0)
    def _():
        update(jd, 0, True)

    @pl.when(jd % 2 == 1)
    def _():
        scores(jd, 1)
        update(jd - 1, 0, False)
        update(jd, 1, True)

    lam = (jnp.exp(jnp.sum(lq1_ref[...] * lk1_ref[...], axis=-1, keepdims=True))
           - jnp.exp(jnp.sum(lq2_ref[...] * lk2_ref[...], axis=-1, keepdims=True)) + lambda_init)
    o = acc_ref[0] * (1.0 / l_ref[0]) - lam * (acc_ref[1] * (1.0 / l_ref[1]))
    g = jnp.concatenate([g_ref[...]] * (tq // HEAD_DIM), axis=1)
    o = o * lax.rsqrt(jnp.mean(o * o, axis=0, keepdims=True) + RMS_EPS) * g
    o_ref[...] = (o * (1.0 - lambda_init)).T.astype(o_ref.dtype)


def _diff_attention(qkv, lq1, lk1, lq2, lk2, g, lambda_init):
    s_len = qkv.shape[0]
    d = qkv.shape[1] // 3
    hw = 2 * HEAD_DIM
    n_heads = d // hw
    tq = tk = min(ATTN_TILE, s_len)
    assert s_len % tk == 0 and tk % tq == 0
    n_kv = s_len // tk
    vt = qkv[:, 2 * d:].reshape(n_kv, tk, n_heads, hw).transpose(2, 0, 3, 1)
    g_cols = jnp.broadcast_to(g.reshape(hw, 1), (hw, HEAD_DIM))
    vec = pl.BlockSpec((1, HEAD_DIM), lambda h, i: (0, 0))
    kv_mode = pl.Buffered(1)
    vmem = 2 * _nbytes((s_len, hw), BF16) + 4 * _nbytes((tq, hw), BF16) \
        + 2 * (4 * _nbytes((V7X_SUBLANES, tq), F32) + _nbytes((tq, hw), F32)) + 16 * _nbytes((tq, tk), F32)
    return pl.pallas_call(
        functools.partial(_diff_attn_kernel, lambda_init=lambda_init, tk=tk),
        grid=(n_heads, s_len // tq),
        in_specs=[pl.BlockSpec((tq, hw), lambda h, i: (i, h)),
                  pl.BlockSpec((s_len, hw), lambda h, i: (0, n_heads + h), pipeline_mode=kv_mode),
                  pl.BlockSpec((None, n_kv, hw, tk), lambda h, i: (h, 0, 0, 0), pipeline_mode=kv_mode),
                  vec, vec, vec, vec,
                  pl.BlockSpec((hw, HEAD_DIM), lambda h, i: (0, 0))],
        out_specs=pl.BlockSpec((tq, hw), lambda h, i: (i, h)),
        out_shape=jax.ShapeDtypeStruct((s_len, d), BF16),
        scratch_shapes=[pltpu.VMEM((2, 1, tq), F32), pltpu.VMEM((2, 1, tq), F32),
                        pltpu.VMEM((2, hw, tq), F32), pltpu.VMEM((2, 2, tk, tq), F32),
                        pltpu.VMEM((2, 2, 1, tq), F32)],
        compiler_params=_params(("parallel", "parallel"), vmem),
        name="diff_attn",
    )(qkv, qkv, vt, lq1, lk1, lq2, lk2, g_cols)


def _rope_tables(positions, q_scale):
    inv = ROPE_THETA ** (-jnp.arange(0, ROT_DIM, 2, dtype=F32) / ROT_DIM)
    ang = positions.astype(F32)[:, None] * inv
    cos, sin = jnp.cos(ang), jnp.sin(ang)
    n = positions.shape[0]
    rest = HEAD_DIM - ROT_DIM
    c = jnp.concatenate([cos, cos, jnp.ones((n, rest), F32)], axis=1)
    s1 = jnp.concatenate([jnp.zeros((n, ROT_HALF), F32), sin, jnp.zeros((n, rest), F32)], axis=1)
    s2 = jnp.concatenate([-sin, jnp.zeros((n, ROT_HALF + rest), F32)], axis=1)
    zero = jnp.zeros_like(c)
    return (jnp.stack([c * q_scale, c, jnp.ones_like(c)]),
            jnp.stack([s1 * q_scale, s1, zero]),
            jnp.stack([s2 * q_scale, s2, zero]))


def kernel(x, positions, ev_w_in, ev_conv_w, ev_conv_b, ev_gate_a_w, ev_gate_a_b, ev_gate_x_w, ev_gate_x_b, ev_rg_lambda, ev_w_out, od_w_in, od_lambda_q1, od_lambda_k1, od_lambda_q2, od_lambda_k2, od_subln_g, od_w_out, ln_mix_g, ln_mix_b, ln_mlp_g, ln_mlp_b, mlp_w1, mlp_w2):
    batch, s_len, d = x.shape
    depth = ln_mix_g.shape[0]
    d_rnn = ev_conv_w.shape[-1]
    d_dil = (ev_w_in.shape[-1] - 2 * d_rnn) // 3
    alpha = (2 * depth) ** 0.25
    q_scale = LOG2E * HEAD_DIM ** -0.5
    row = lambda v: v.reshape(1, -1)

    ev_w_in_b, ev_w_out_b = ev_w_in.astype(BF16), ev_w_out.astype(BF16)
    od_w_in_b, od_w_out_b = od_w_in.astype(BF16), od_w_out.astype(BF16)
    mlp_w1_b, mlp_w2_b = mlp_w1.astype(BF16), mlp_w2.astype(BF16)
    dils = [dil for _, dil in DIL_PATTERNS]

    outs = []
    for bi in range(batch):
        xf = x[bi]
        xb = xf.astype(BF16)
        tabs = _rope_tables(positions[bi], q_scale)
        for layer in range(depth):
            p = layer // 2
            if layer % 2 == 0:
                ga = _proj(xb, ev_w_in_b, p, 0, 2 * d_rnn, F32)
                qkvs = _proj_rope(xb, ev_w_in_b, p, 2 * d_rnn, 3 * d_dil, tabs, d_dil, 2 * d_dil,
                                  [dil for dil in dils if dil > 1])
                qkv_by_dil = dict(zip(dils, [qkvs[0][None]] + list(qkvs[1:])))
                wa = _block_diag_groups(ev_gate_a_w[p], V7X_MXU_DIM).astype(BF16)
                wx = _block_diag_groups(ev_gate_x_w[p], V7X_MXU_DIM).astype(BF16)
                ya = _rglru(ga, ev_conv_w[p], row(ev_conv_b[p]), wa, wx, row(ev_gate_a_b[p]),
                            row(ev_gate_x_b[p]), row(ev_rg_lambda[p]))
                yb = _dilated_attention(qkv_by_dil)
                ys, w_out = [ya, yb], ev_w_out_b
            else:
                lambda_init = 0.8 - 0.6 * math.exp(-0.3 * layer)
                (qkv,) = _proj_rope(xb, od_w_in_b, p, 0, 3 * d, tabs, d, 2 * d)
                y = _diff_attention(qkv, row(od_lambda_q1[p]), row(od_lambda_k1[p]), row(od_lambda_q2[p]),
                                    row(od_lambda_k2[p]), row(od_subln_g[p]), lambda_init)
                ys, w_out = [y], od_w_out_b
            xf, xb = _outproj_ln(ys, w_out, p, xf, row(ln_mix_g[layer]), row(ln_mix_b[layer]), alpha)
            xf, xb = _mlp_ln(xb, mlp_w1_b, mlp_w2_b, layer, xf, row(ln_mlp_g[layer]), row(ln_mlp_b[layer]), alpha)
        outs.append(xf)
    return outs[0][None] if batch == 1 else jnp.stack(outs, axis=0)
```

```python
import functools
import math

import jax
import jax.numpy as jnp
from jax import lax
from jax.experimental import pallas as pl
from jax.experimental.pallas import tpu as pltpu

HEAD_DIM = 128
ROT_DIM = HEAD_DIM // 4
ROT_HALF = ROT_DIM // 2
ROPE_THETA = 500000.0
CONV_W = 4
RG_C = 8.0
RNN_BLOCK_W = 64
DIL_PATTERNS = ((128, 1), (512, 4), (2048, 16))
BLK = 128
LN_EPS = 1e-5
RMS_EPS = 1e-5
NEG = -1e30
LOG2E = 1.4426950408889634

V7X_VMEM_BYTES = 64 * 2**20
V7X_VMEM_RESERVED_BYTES = 8 * 2**20
V7X_SUBLANES = 8
V7X_MXU_DIM = 256

PROJ_ROWS = 1024
PROJ_COLS = 1024
PROJ_COLS_WIDE = 2048
OUTPROJ_ROWS = 512
MLP_ROWS = 512
MLP_HIDDEN_COLS = 1024
LN_SUB_ROWS = V7X_MXU_DIM
RGLRU_ROWS = 256
DIL_ROWS = 512
ATTN_TILE = 1024

F32 = jnp.float32
BF16 = jnp.bfloat16


def _params(semantics, vmem_bytes):
    limit = min(int(vmem_bytes), V7X_VMEM_BYTES - V7X_VMEM_RESERVED_BYTES)
    return pltpu.CompilerParams(dimension_semantics=semantics, vmem_limit_bytes=limit)


def _nbytes(shape, dtype):
    return math.prod(shape) * jnp.dtype(dtype).itemsize


def _ln_rows(z, g, b):
    mu = jnp.mean(z, axis=-1, keepdims=True)
    zc = z - mu
    var = jnp.mean(zc * zc, axis=-1, keepdims=True)
    return zc * lax.rsqrt(var + LN_EPS) * g + b


def _proj_kernel(x_ref, w_ref, o_ref):
    o_ref[...] = jnp.dot(x_ref[...], w_ref[...], preferred_element_type=F32).astype(o_ref.dtype)


def _proj_rope_kernel(x_ref, w_ref, c_ref, s1_ref, s2_ref, o_ref, *rest, dilations):
    dil_refs, slab_refs = rest[:len(dilations)], rest[len(dilations):]
    tm = x_ref.shape[0]
    acc = jnp.dot(x_ref[...], w_ref[...], preferred_element_type=F32)
    c, s1, s2 = c_ref[...], s1_ref[...], s2_ref[...]
    for h in range(acc.shape[1] // HEAD_DIM):
        cols = slice(h * HEAD_DIM, (h + 1) * HEAD_DIM)
        a = acc[:, cols]
        r = a * c + pltpu.roll(a, ROT_HALF, 1) * s1 + pltpu.roll(a, HEAD_DIM - ROT_HALF, 1) * s2
        o_ref[:, cols] = r.astype(o_ref.dtype)
        if dilations:
            base, src_ref, src_d = h * tm, slab_refs[0], 1
            src_ref[base:base + tm, :] = r
            for idx, (d, d_ref) in enumerate(zip(dilations, dil_refs)):
                f, n_rows = d // src_d, tm // d
                dst_ref = slab_refs[(idx + 1) % 2]
                for rho_src in range(src_d):
                    for k in range(f):
                        rho = k * src_d + rho_src
                        val = src_ref[pl.ds(base + rho_src * (tm // src_d) + k, n_rows, stride=f), :]
                        d_ref[rho, :, cols] = val.astype(d_ref.dtype)
                        if idx + 1 < len(dilations):
                            dst_ref[base + rho * n_rows:base + (rho + 1) * n_rows, :] = val
                src_ref, src_d = dst_ref, d


def _proj_tiles(m, n, wide):
    tm = min(PROJ_ROWS, m)
    tn = min(PROJ_COLS_WIDE if wide else PROJ_COLS, n)
    assert m % tm == 0 and n % tn == 0
    return tm, tn


def _proj(xb, w, layer, col0, n, out_dtype):
    m, k = xb.shape
    tm, tn = _proj_tiles(m, n, wide=True)
    assert col0 % tn == 0
    vmem = 2 * (_nbytes((tm, k), BF16) + _nbytes((k, tn), BF16) + _nbytes((tm, tn), out_dtype)) \
        + 2 * _nbytes((tm, tn), F32)
    return pl.pallas_call(
        _proj_kernel,
        grid=(m // tm, n // tn),
        in_specs=[pl.BlockSpec((tm, k), lambda i, j: (i, 0)),
                  pl.BlockSpec((None, k, tn), lambda i, j: (layer, 0, j + col0 // tn))],
        out_specs=pl.BlockSpec((tm, tn), lambda i, j: (i, j)),
        out_shape=jax.ShapeDtypeStruct((m, n), out_dtype),
        compiler_params=_params(("parallel", "arbitrary"), vmem),
        name="proj",
    )(xb, w)


def _proj_rope(xb, w, layer, col0, n, rope_tabs, n_q_cols, n_rope_cols, dilations=()):
    m, k = xb.shape
    tm, tn = _proj_tiles(m, n, wide=not dilations)
    assert n_q_cols % tn == 0 and n_rope_cols % tn == 0 and col0 % tn == 0
    n_q, n_rope = n_q_cols // tn, n_rope_cols // tn
    variant = lambda j: jnp.where(j < n_q, 0, jnp.where(j < n_rope, 1, 2))
    tab_spec = pl.BlockSpec((None, tm, HEAD_DIM), lambda i, j: (variant(j), i, 0))
    out_specs = [pl.BlockSpec((tm, tn), lambda i, j: (i, j))]
    out_shape = [jax.ShapeDtypeStruct((m, n), BF16)]
    for prev_d, d in zip((1,) + tuple(dilations), dilations):
        assert d % prev_d == 0 and tm % (d * 2 * V7X_SUBLANES) == 0
        out_specs.append(pl.BlockSpec((d, tm // d, tn), lambda i, j: (0, i, j)))
        out_shape.append(jax.ShapeDtypeStruct((d, m // d, n), BF16))
    scratch = [pltpu.VMEM((tn // HEAD_DIM * tm, HEAD_DIM), F32)] * (2 if dilations else 0)
    vmem = 2 * (_nbytes((tm, k), BF16) + _nbytes((k, tn), BF16) + (1 + len(dilations)) * _nbytes((tm, tn), BF16)
                + 3 * _nbytes((tm, HEAD_DIM), F32)) + (3 + 2 * bool(dilations)) * _nbytes((tm, tn), F32)
    return pl.pallas_call(
        functools.partial(_proj_rope_kernel, dilations=tuple(dilations)),
        grid=(m // tm, n // tn),
        in_specs=[pl.BlockSpec((tm, k), lambda i, j: (i, 0)),
                  pl.BlockSpec((None, k, tn), lambda i, j: (layer, 0, j + col0 // tn)),
                  tab_spec, tab_spec, tab_spec],
        out_specs=out_specs,
        out_shape=out_shape,
        scratch_shapes=scratch,
        compiler_params=_params(("parallel", "arbitrary"), vmem),
        name="proj_rope",
    )(xb, w, *rope_tabs)


def _outproj_ln_kernel(*refs, n_in, alpha, sub):
    ys, ws = refs[:n_in], refs[n_in:2 * n_in]
    x_ref, g_ref, b_ref, of_ref, ob_ref = refs[2 * n_in:]
    for r in range(x_ref.shape[0] // sub):
        rows = slice(r * sub, (r + 1) * sub)
        acc = jnp.dot(ys[0][rows, :], ws[0][...], preferred_element_type=F32)
        for y_ref, w_ref in zip(ys[1:], ws[1:]):
            acc = acc + jnp.dot(y_ref[rows, :], w_ref[...], preferred_element_type=F32)
        y = _ln_rows(alpha * x_ref[rows, :] + acc, g_ref[...], b_ref[...])
        of_ref[rows, :] = y
        ob_ref[rows, :] = y.astype(BF16)


def _outproj_ln(ys, w, layer, xf, g, b, alpha):
    m, d = xf.shape
    tm = min(OUTPROJ_ROWS, m)
    assert m % tm == 0
    row = lambda i: (i, 0)
    const = lambda i: (0, 0)
    in_specs = [pl.BlockSpec((tm, y.shape[1]), row) for y in ys]
    row0 = 0
    for y in ys:
        rows = y.shape[1]
        assert row0 % rows == 0
        in_specs.append(pl.BlockSpec((None, rows, d), lambda i, blk=row0 // rows: (layer, blk, 0)))
        row0 += rows
    assert row0 == w.shape[1]
    in_specs += [pl.BlockSpec((tm, d), row), pl.BlockSpec((1, d), const), pl.BlockSpec((1, d), const)]
    vmem = 2 * (sum(_nbytes((tm, y.shape[1]), BF16) for y in ys) + _nbytes(w.shape[1:], BF16)
                + 2 * _nbytes((tm, d), F32) + _nbytes((tm, d), BF16)) + 3 * _nbytes((tm, d), F32)
    return pl.pallas_call(
        functools.partial(_outproj_ln_kernel, n_in=len(ys), alpha=alpha, sub=min(LN_SUB_ROWS, tm)),
        grid=(m // tm,),
        in_specs=in_specs,
        out_specs=[pl.BlockSpec((tm, d), row), pl.BlockSpec((tm, d), row)],
        out_shape=[jax.ShapeDtypeStruct((m, d), F32), jax.ShapeDtypeStruct((m, d), BF16)],
        compiler_params=_params(("parallel",), vmem),
        name="outproj_ln",
    )(*ys, *([w] * len(ys)), xf, g, b)


def _mlp_ln_kernel(xb_ref, w1_ref, w2_ref, x_ref, g_ref, b_ref, of_ref, ob_ref, acc_ref, *, alpha, sub):
    j = pl.program_id(1)
    last = pl.num_programs(1) - 1
    acc = acc_ref.at[pl.program_id(2)]

    def hidden():
        h = jnp.maximum(jnp.dot(xb_ref[...], w1_ref[...], preferred_element_type=F32), 0.0)
        return (h * h).astype(BF16)

    @pl.when(j == 0)
    def _():
        acc[...] = jnp.zeros_like(acc)

    @pl.when(j < last)
    def _():
        acc[...] += jnp.dot(hidden(), w2_ref[...], preferred_element_type=F32)

    @pl.when(j == last)
    def _():
        h = hidden()
        for r in range(x_ref.shape[0] // sub):
            rows = slice(r * sub, (r + 1) * sub)
            z = acc[rows, :] + jnp.dot(h[rows, :], w2_ref[...], preferred_element_type=F32)
            y = _ln_rows(alpha * x_ref[rows, :] + z, g_ref[...], b_ref[...])
            of_ref[rows, :] = y
            ob_ref[rows, :] = y.astype(BF16)


def _mlp_ln(xb, w1, w2, layer, xf, g, b, alpha):
    m, d = xf.shape
    f = w1.shape[2]
    tm = min(MLP_ROWS, m // 2)
    tf = min(MLP_HIDDEN_COLS, f)
    assert m % (2 * tm) == 0 and f % tf == 0
    n_chunks = f // tf
    late = lambda i, j, r: (2 * i + jnp.where(j == n_chunks - 1, r, 0), 0)
    const = lambda i, j, r: (0, 0)
    vmem = 2 * (_nbytes((tm, d), BF16) + 2 * _nbytes((d, tf), BF16) + 2 * _nbytes((tm, d), F32)
                + _nbytes((tm, d), BF16)) + 3 * _nbytes((tm, d), F32) + 4 * _nbytes((tm, tf), F32)
    return pl.pallas_call(
        functools.partial(_mlp_ln_kernel, alpha=alpha, sub=min(LN_SUB_ROWS, tm)),
        grid=(m // (2 * tm), n_chunks, 2),
        in_specs=[pl.BlockSpec((tm, d), lambda i, j, r: (2 * i + r, 0)),
                  pl.BlockSpec((None, d, tf), lambda i, j, r: (layer, 0, j)),
                  pl.BlockSpec((None, tf, d), lambda i, j, r: (layer, j, 0)),
                  pl.BlockSpec((tm, d), late),
                  pl.BlockSpec((1, d), const), pl.BlockSpec((1, d), const)],
        out_specs=[pl.BlockSpec((tm, d), late), pl.BlockSpec((tm, d), late)],
        out_shape=[jax.ShapeDtypeStruct((m, d), F32), jax.ShapeDtypeStruct((m, d), BF16)],
        scratch_shapes=[pltpu.VMEM((2, tm, d), F32)],
        compiler_params=_params(("parallel", "arbitrary", "arbitrary"), vmem),
        name="mlp_ln",
    )(xb, w1, w2, xf, g, b)


def _rglru_kernel(ag_ref, ax_ref, cw_ref, cb_ref, wa_ref, wx_ref, ba_ref, bx_ref, lam_ref,
                  o_ref, xs_ref, a_ref, b_ref, h_ref):
    i = pl.program_id(0)
    tm, c = ax_ref.shape
    sub = V7X_SUBLANES
    gw = wa_ref.shape[1]

    @pl.when(i == 0)
    def _():
        xs_ref[0:sub, :] = jnp.zeros((sub, c), F32)
        h_ref[...] = jnp.zeros_like(h_ref)

    x = ax_ref[...]
    xs_ref[sub:sub + tm, :] = x
    xc = cb_ref[...] + cw_ref[CONV_W - 1:CONV_W, :] * x
    for k in range(1, CONV_W):
        xc = xc + cw_ref[CONV_W - 1 - k:CONV_W - k, :] * xs_ref[pl.ds(sub - k, tm), :]
    xs_ref[0:sub, :] = x[tm - sub:tm, :]

    xcb = xc.astype(BF16)
    row_in_tile = lax.broadcasted_iota(jnp.int32, (tm, gw), 0) % sub
    for g in range(c // gw):
        cols = slice(g * gw, (g + 1) * gw)
        r = jax.nn.sigmoid(jnp.dot(xcb[:, cols], wa_ref[g], preferred_element_type=F32) + ba_ref[:, cols])
        ig = jax.nn.sigmoid(jnp.dot(xcb[:, cols], wx_ref[g], preferred_element_type=F32) + bx_ref[:, cols])
        log_a = (-RG_C * r) * jax.nn.softplus(-lam_ref[:, cols])
        a = jnp.exp(log_a)
        u = jnp.sqrt(-jnp.tanh(log_a) * (a * a + 1.0)) * (ig * xc[:, cols])
        for s in (1, 2, 4):
            keep = row_in_tile >= s
            u = jnp.where(keep, a * pltpu.roll(u, s, 0) + u, u)
            a = jnp.where(keep, a * pltpu.roll(a, s, 0), a)
        a_ref[:, cols] = a
        b_ref[:, cols] = u

    def tile_body(t, h):
        rows = pl.ds(pl.multiple_of(t * sub, sub), sub)
        hr = a_ref[rows, :] * h + b_ref[rows, :]
        b_ref[rows, :] = hr
        return jnp.broadcast_to(hr[sub - 1:sub, :], hr.shape)

    h_ref[...] = lax.fori_loop(0, tm // sub, tile_body, h_ref[...], unroll=4)

    gate = ag_ref[...]
    cdf = 0.5 * (1.0 + jnp.tanh(math.sqrt(2.0 / math.pi) * (gate + 0.044715 * (gate * gate * gate))))
    o_ref[...] = (gate * cdf * b_ref[...]).astype(o_ref.dtype)


def _rglru(ga, conv_w, conv_b, wa, wx, ba, bx, lam):
    m = ga.shape[0]
    c = ga.shape[1] // 2
    tm = min(RGLRU_ROWS, m)
    assert m % tm == 0
    gw = wa.shape[1]
    const2 = lambda i: (0, 0)
    vmem = 2 * (2 * _nbytes((tm, c), F32) + _nbytes((tm, c), BF16) + 2 * _nbytes(wa.shape, BF16)) \
        + 3 * _nbytes((tm + V7X_SUBLANES, c), F32) + 12 * _nbytes((tm, c), F32)
    return pl.pallas_call(
        _rglru_kernel,
        grid=(m // tm,),
        in_specs=[pl.BlockSpec((tm, c), lambda i: (i, 0)),
                  pl.BlockSpec((tm, c), lambda i: (i, 1)),
                  pl.BlockSpec((CONV_W, c), const2), pl.BlockSpec((1, c), const2),
                  pl.BlockSpec(wa.shape, lambda i: (0, 0, 0)), pl.BlockSpec(wx.shape, lambda i: (0, 0, 0)),
                  pl.BlockSpec((1, c), const2), pl.BlockSpec((1, c), const2), pl.BlockSpec((1, c), const2)],
        out_specs=pl.BlockSpec((tm, c), lambda i: (i, 0)),
        out_shape=jax.ShapeDtypeStruct((m, c), BF16),
        scratch_shapes=[pltpu.VMEM((tm + V7X_SUBLANES, c), F32), pltpu.VMEM((tm, c), F32),
                        pltpu.VMEM((tm, c), F32), pltpu.VMEM((V7X_SUBLANES, c), F32)],
        compiler_params=_params(("arbitrary",), vmem),
        name="rglru",
    )(ga, ga, conv_w, conv_b, wa, wx, ba, bx, lam)


def _block_diag_groups(w, group):
    n, bw, _ = w.shape
    per = group // bw
    eye = jnp.eye(per, dtype=w.dtype)
    wg = w.reshape(n // per, per, bw, bw)
    return jnp.einsum('gpab,pq->gpaqb', wg, eye).reshape(n // per, group, group)


def _dil_kernel(*refs, has_prev, regroup):
    q_ref, kc_ref, kp_ref, vc_ref, vp_ref = refs[:5]
    refs = refs[5:]
    if has_prev:
        op_ref, lp_ref = refs[:2]
        refs = refs[2:]
    if regroup:
        o_ref, l_ref, oslab_ref, lslab_ref = refs
    else:
        (o_ref,) = refs

    i = pl.program_id(1)
    tq = q_ref.shape[0]
    n_heads = q_ref.shape[1] // HEAD_DIM
    qi = lax.broadcasted_iota(jnp.int32, (BLK, 2 * BLK), 0)
    kj = lax.broadcasted_iota(jnp.int32, (BLK, 2 * BLK), 1)
    band = (kj >= qi) & (kj <= qi + BLK)
    lane = lax.broadcasted_iota(jnp.int32, (BLK, HEAD_DIM), 1)
    contract_last = (((1,), (1,)), ((), ()))
    for b in range(tq // BLK):
        rows = slice(b * BLK, (b + 1) * BLK)
        valid = band & (kj >= jnp.where(i > 0, 0, BLK)) if b == 0 else band
        bias = jnp.where(valid, 0.0, NEG)
        lp_all = lp_ref[rows, :] if has_prev else None
        lse_all = jnp.zeros((BLK, HEAD_DIM), F32)
        for h in range(n_heads):
            cols = slice(h * HEAD_DIM, (h + 1) * HEAD_DIM)
            if b == 0:
                kk = jnp.concatenate([kp_ref[:, cols], kc_ref[0:BLK, cols]], axis=0)
                vv = jnp.concatenate([vp_ref[:, cols], vc_ref[0:BLK, cols]], axis=0)
            else:
                kk = kc_ref[(b - 1) * BLK:(b + 1) * BLK, cols]
                vv = vc_ref[(b - 1) * BLK:(b + 1) * BLK, cols]
            s = lax.dot_general(q_ref[rows, cols], kk, contract_last, preferred_element_type=F32) + bias
            m = jnp.max(s, axis=-1, keepdims=True)
            p = jnp.exp2(s - m)
            l = jnp.sum(p, axis=-1, keepdims=True)
            o = jnp.dot(p.astype(BF16), vv, preferred_element_type=F32) / l
            lse = m + jnp.log2(l)
            if has_prev:
                lp = jnp.sum(jnp.where(lane == h, lp_all, 0.0), axis=-1, keepdims=True)
                mx = jnp.maximum(lp, lse)
                wa = jnp.exp2(lp - mx)
                wb = jnp.exp2(lse - mx)
                tot = wa + wb
                o = (op_ref[rows, cols] * wa + o * wb) / tot
                lse = mx + jnp.log2(tot)
            if regroup:
                lse_all = jnp.where(lane == h, lse, lse_all)
                base = (b * n_heads + h) * BLK
                oslab_ref[base:base + BLK, :] = o
                part = BLK // regroup
                for c in range(regroup):
                    o_ref[c, b * part:(b + 1) * part, cols] = oslab_ref[pl.ds(base + c, part, stride=regroup), :]
            else:
                o_ref[rows, cols] = o.astype(o_ref.dtype)
        if regroup:
            lslab_ref[b * BLK:(b + 1) * BLK, :] = lse_all
            for c in range(regroup):
                l_ref[c, b * part:(b + 1) * part, :] = lslab_ref[pl.ds(b * BLK + c, part, stride=regroup), :]


def _dilated_attention(qkv_by_dil):
    dils = [dil for _, dil in DIL_PATTERNS]
    s_len = qkv_by_dil[dils[0]].shape[1] * dils[0]
    c = qkv_by_dil[dils[0]].shape[2] // 3
    o = lse = None
    for idx, (window, dil) in enumerate(DIL_PATTERNS):
        assert window // dil == BLK
        has_prev, is_last = idx > 0, idx == len(DIL_PATTERNS) - 1
        regroup = 0 if is_last else dils[idx + 1] // dil
        length = s_len // dil
        tq = min(DIL_ROWS, length)
        assert length % tq == 0 and tq % BLK == 0
        per = tq // BLK
        cur = lambda off: pl.BlockSpec((None, tq, c), lambda r, i, off=off: (r, i, off))
        prev = lambda off: pl.BlockSpec((None, BLK, c),
                                        lambda r, i, off=off: (r, jnp.maximum(i * per - 1, 0), off))
        in_specs = [cur(0), cur(1), prev(1), cur(2), prev(2)]
        args = [qkv_by_dil[dil]] * 5
        if has_prev:
            in_specs += [cur(0), pl.BlockSpec((None, tq, HEAD_DIM), lambda r, i: (r, i, 0))]
            args += [o, lse]
        scratch = []
        if is_last:
            out_specs = [pl.BlockSpec((None, tq, c), lambda r, i: (r, i, 0))]
            out_shape = [jax.ShapeDtypeStruct((dil, length, c), BF16)]
        else:
            assert dils[idx + 1] == regroup * dil and BLK % (regroup * V7X_SUBLANES) == 0
            grouped = lambda w: pl.BlockSpec((regroup, None, tq // regroup, w), lambda r, i: (0, r, i, 0))
            out_specs = [grouped(c), grouped(HEAD_DIM)]
            out_shape = [jax.ShapeDtypeStruct((regroup, dil, length // regroup, w), F32) for w in (c, HEAD_DIM)]
            scratch = [pltpu.VMEM((per * (c // HEAD_DIM) * BLK, HEAD_DIM), F32), pltpu.VMEM((tq, HEAD_DIM), F32)]
        vmem = 2 * (3 * _nbytes((tq, c), BF16) + 2 * _nbytes((BLK, c), BF16) + 4 * _nbytes((tq, c), F32)) \
            + 2 * _nbytes((tq, c), F32) + 16 * _nbytes((BLK, 2 * BLK), F32) * (c // HEAD_DIM)
        res = pl.pallas_call(
            functools.partial(_dil_kernel, has_prev=has_prev, regroup=regroup),
            grid=(dil, length // tq),
            in_specs=in_specs,
            out_specs=out_specs,
            out_shape=out_shape,
            scratch_shapes=scratch,
            compiler_params=_params(("parallel", "parallel"), vmem),
            name=f"dilated_attn_d{dil}",
        )(*args)
        if is_last:
            return res[0].transpose(1, 0, 2).reshape(s_len, c)
        o, lse = (r.reshape(regroup * dil, length // regroup, r.shape[-1]) for r in res)


def _diff_attn_kernel(q_ref, k_ref, vt_ref, lq1_ref, lk1_ref, lq2_ref, lk2_ref, g_ref, o_ref,
                      m_ref, l_ref, acc_ref, s_ref, smax_ref, *, lambda_init, tk):
    tq = q_ref.shape[0]
    assert tk % tq == 0
    q0 = pl.program_id(1) * tq
    jd = q0 // tk
    m_ref[...] = jnp.full_like(m_ref, NEG)
    l_ref[...] = jnp.zeros_like(l_ref)
    acc_ref[...] = jnp.zeros_like(acc_ref)
    contract_last = (((1,), (1,)), ((), ()))

    def scores(j, slot):
        kb = k_ref[pl.ds(pl.multiple_of(j * tk, tk), tk), :]
        for c in range(2):
            cols = slice(c * HEAD_DIM, (c + 1) * HEAD_DIM)
            s = lax.dot_general(kb[:, cols], q_ref[:, cols], contract_last, preferred_element_type=F32)
            s_ref[slot, c] = s
            smax_ref[slot, c] = jnp.max(s, axis=0, keepdims=True)

    def absorb(c, s, m_cur, vt, q_lo, k_lo):
        qs = slice(q_lo, tq)
        m_prev = m_ref[c, :, qs]
        m_new = jnp.maximum(m_prev, m_cur)
        alpha = jnp.exp2(m_prev - m_new)
        p = jnp.exp2(s - m_new)
        l_ref[c, :, qs] = alpha * l_ref[c, :, qs] + jnp.sum(p, axis=0, keepdims=True)
        m_ref[c, :, qs] = m_new
        pv = jnp.dot(vt[:, k_lo:k_lo + s.shape[0]], p.astype(BF16), preferred_element_type=F32)
        acc_ref[c, :, qs] = acc_ref[c, :, qs] * alpha + pv

    def update(j, slot, masked):
        vt = vt_ref[j]
        for c in range(2):
            if not masked:
                absorb(c, s_ref[slot, c], smax_ref[slot, c], vt, 0, 0)
                continue
            parts = [(0, tk // 2, 0), (tk // 2, tk, tq // 2)] if tq == tk else [(0, tk, 0)]
            for k_lo, k_hi, q_lo in parts:
                s = s_ref[slot, c, k_lo:k_hi, q_lo:tq]
                kj = lax.broadcasted_iota(jnp.int32, s.shape, 0) + k_lo
                qi = lax.broadcasted_iota(jnp.int32, s.shape, 1) + q_lo
                s = jnp.where(kj <= qi + (q0 - jd * tk), s, NEG)
                absorb(c, s, jnp.max(s, axis=0, keepdims=True), vt, q_lo, k_lo)

    scores(0, 0)

    def pair(p, carry):
        scores(2 * p + 1, 1)
        update(2 * p, 0, False)
        scores(2 * p + 2, 0)
        update(2 * p + 1, 1, False)
        return carry

    lax.fori_loop(0, jd // 2, pair, 0)

    @pl.when(jd % 2 == 0)
    def _():
        update(jd, 0, True)

    @pl.when(jd % 2 == 1)
    def _():
        scores(jd, 1)
        update(jd - 1, 0, False)
        update(jd, 1, True)

    lam = (jnp.exp(jnp.sum(lq1_ref[...] * lk1_ref[...], axis=-1, keepdims=True))
           - jnp.exp(jnp.sum(lq2_ref[...] * lk2_ref[...], axis=-1, keepdims=True)) + lambda_init)
    o = acc_ref[0] * (1.0 / l_ref[0]) - lam * (acc_ref[1] * (1.0 / l_ref[1]))
    g = jnp.concatenate([g_ref[...]] * (tq // HEAD_DIM), axis=1)
    o = o * lax.rsqrt(jnp.mean(o * o, axis=0, keepdims=True) + RMS_EPS) * g
    o_ref[...] = (o * (1.0 - lambda_init)).T.astype(o_ref.dtype)


def _diff_attention(qkv, lq1, lk1, lq2, lk2, g, lambda_init):
    s_len = qkv.shape[0]
    d = qkv.shape[1] // 3
    hw = 2 * HEAD_DIM
    n_heads = d // hw
    tq = tk = min(ATTN_TILE, s_len)
    assert s_len % tk == 0 and tk % tq == 0
    n_kv = s_len // tk
    vt = qkv[:, 2 * d:].reshape(n_kv, tk, n_heads, hw).transpose(2, 0, 3, 1)
    g_cols = jnp.broadcast_to(g.reshape(hw, 1), (hw, HEAD_DIM))
    vec = pl.BlockSpec((1, HEAD_DIM), lambda h, i: (0, 0))
    kv_mode = pl.Buffered(1)
    vmem = 2 * _nbytes((s_len, hw), BF16) + 4 * _nbytes((tq, hw), BF16) \
        + 2 * (4 * _nbytes((V7X_SUBLANES, tq), F32) + _nbytes((tq, hw), F32)) + 16 * _nbytes((tq, tk), F32)
    return pl.pallas_call(
        functools.partial(_diff_attn_kernel, lambda_init=lambda_init, tk=tk),
        grid=(n_heads, s_len // tq),
        in_specs=[pl.BlockSpec((tq, hw), lambda h, i: (i, h)),
                  pl.BlockSpec((s_len, hw), lambda h, i: (0, n_heads + h), pipeline_mode=kv_mode),
                  pl.BlockSpec((None, n_kv, hw, tk), lambda h, i: (h, 0, 0, 0), pipeline_mode=kv_mode),
                  vec, vec, vec, vec,
                  pl.BlockSpec((hw, HEAD_DIM), lambda h, i: (0, 0))],
        out_specs=pl.BlockSpec((tq, hw), lambda h, i: (i, h)),
        out_shape=jax.ShapeDtypeStruct((s_len, d), BF16),
        scratch_shapes=[pltpu.VMEM((2, 1, tq), F32), pltpu.VMEM((2, 1, tq), F32),
                        pltpu.VMEM((2, hw, tq), F32), pltpu.VMEM((2, 2, tk, tq), F32),
                        pltpu.VMEM((2, 2, 1, tq), F32)],
        compiler_params=_params(("parallel", "parallel"), vmem),
        name="diff_attn",
    )(qkv, qkv, vt, lq1, lk1, lq2, lk2, g_cols)


def _rope_tables(positions, q_scale):
    inv = ROPE_THETA ** (-jnp.arange(0, ROT_DIM, 2, dtype=F32) / ROT_DIM)
    ang = positions.astype(F32)[:, None] * inv
    cos, sin = jnp.cos(ang), jnp.sin(ang)
    n = positions.shape[0]
    rest = HEAD_DIM - ROT_DIM
    c = jnp.concatenate([cos, cos, jnp.ones((n, rest), F32)], axis=1)
    s1 = jnp.concatenate([jnp.zeros((n, ROT_HALF), F32), sin, jnp.zeros((n, rest), F32)], axis=1)
    s2 = jnp.concatenate([-sin, jnp.zeros((n, ROT_HALF + rest), F32)], axis=1)
    zero = jnp.zeros_like(c)
    return (jnp.stack([c * q_scale, c, jnp.ones_like(c)]),
            jnp.stack([s1 * q_scale, s1, zero]),
            jnp.stack([s2 * q_scale, s2, zero]))


def kernel(x, positions, ev_w_in, ev_conv_w, ev_conv_b, ev_gate_a_w, ev_gate_a_b, ev_gate_x_w, ev_gate_x_b, ev_rg_lambda, ev_w_out, od_w_in, od_lambda_q1, od_lambda_k1, od_lambda_q2, od_lambda_k2, od_subln_g, od_w_out, ln_mix_g, ln_mix_b, ln_mlp_g, ln_mlp_b, mlp_w1, mlp_w2):
    batch, s_len, d = x.shape
    depth = ln_mix_g.shape[0]
    d_rnn = ev_conv_w.shape[-1]
    d_dil = (ev_w_in.shape[-1] - 2 * d_rnn) // 3
    alpha = (2 * depth) ** 0.25
    q_scale = LOG2E * HEAD_DIM ** -0.5
    row = lambda v: v.reshape(1, -1)

    ev_w_in_b, ev_w_out_b = ev_w_in.astype(BF16), ev_w_out.astype(BF16)
    od_w_in_b, od_w_out_b = od_w_in.astype(BF16), od_w_out.astype(BF16)
    mlp_w1_b, mlp_w2_b = mlp_w1.astype(BF16), mlp_w2.astype(BF16)
    dils = [dil for _, dil in DIL_PATTERNS]

    outs = []
    for bi in range(batch):
        xf = x[bi]
        xb = xf.astype(BF16)
        tabs = _rope_tables(positions[bi], q_scale)
        for layer in range(depth):
            p = layer // 2
            if layer % 2 == 0:
                ga = _proj(xb, ev_w_in_b, p, 0, 2 * d_rnn, F32)
                qkvs = _proj_rope(xb, ev_w_in_b, p, 2 * d_rnn, 3 * d_dil, tabs, d_dil, 2 * d_dil,
                                  [dil for dil in dils if dil > 1])
                qkv_by_dil = dict(zip(dils, [qkvs[0][None]] + list(qkvs[1:])))
                wa = _block_diag_groups(ev_gate_a_w[p], V7X_MXU_DIM).astype(BF16)
                wx = _block_diag_groups(ev_gate_x_w[p], V7X_MXU_DIM).astype(BF16)
                ya = _rglru(ga, ev_conv_w[p], row(ev_conv_b[p]), wa, wx, row(ev_gate_a_b[p]),
                            row(ev_gate_x_b[p]), row(ev_rg_lambda[p]))
                yb = _dilated_attention(qkv_by_dil)
                ys, w_out = [ya, yb], ev_w_out_b
            else:
                lambda_init = 0.8 - 0.6 * math.exp(-0.3 * layer)
                (qkv,) = _proj_rope(xb, od_w_in_b, p, 0, 3 * d, tabs, d, 2 * d)
                y = _diff_attention(qkv, row(od_lambda_q1[p]), row(od_lambda_k1[p]), row(od_lambda_q2[p]),
                                    row(od_lambda_k2[p]), row(od_subln_g[p]), lambda_init)
                ys, w_out = [y], od_w_out_b
            xf, xb = _outproj_ln(ys, w_out, p, xf, row(ln_mix_g[layer]), row(ln_mix_b[layer]), alpha)
            xf, xb = _mlp_ln(xb, mlp_w1_b, mlp_w2_b, layer, xf, row(ln_mlp_g[layer]), row(ln_mlp_b[layer]), alpha)
        outs.append(xf)
    return outs[0][None] if batch == 1 else jnp.stack(outs, axis=0)
```
